```python
import jax, jax.numpy as jnp
from jax import lax
import numpy as np

D_MODEL = 1024
BATCH = 4
SEQ = 4096
DEPTH = 2

N_MEM = 256
EPS = 1e-6
ROPE_THETA = 500000.0

SSD_HEADS = 8
SSD_HEAD_DIM = 64
SSD_INNER = SSD_HEADS * SSD_HEAD_DIM
SSD_GROUPS = 2
SSD_STATE = 128
SSD_CONV = 4
SSD_CHUNK = 128
SSD_CONV_DIM = SSD_INNER + 2 * SSD_GROUPS * SSD_STATE

HEAD_DIM = 64
ATTN_HEADS = 4
ATTN_INNER = ATTN_HEADS * HEAD_DIM
IDX_HEADS = 4
IDX_DIM = 64
TOPK_MAX = 256
Q_BLOCK = 128

MEM_HEADS = 4
MEM_INNER = MEM_HEADS * HEAD_DIM

D_MIX = SSD_INNER + ATTN_INNER + MEM_INNER

D_FF = 2816
FFN_CONV = 3

IN_SIZES = (SSD_INNER, SSD_CONV_DIM, SSD_HEADS,
            ATTN_INNER, ATTN_INNER, ATTN_INNER, IDX_HEADS * IDX_DIM, IDX_DIM, IDX_HEADS,
            MEM_INNER)
N_IN = sum(IN_SIZES)
IN_SPLITS = tuple(sum(IN_SIZES[:i + 1]) for i in range(len(IN_SIZES) - 1))

kernel_name = "hybrid_ssd_dsa_mem_convffn"


def rmsnorm(x, g):
    xf = x.astype(jnp.float32)
    r = lax.rsqrt(jnp.mean(xf * xf, axis=-1, keepdims=True) + EPS)
    return (xf * r).astype(x.dtype) * g


def rope_partial(x, pos):
    rot = x.shape[-1] // 4
    half = rot // 2
    inv_freq = ROPE_THETA ** (-(jnp.arange(half, dtype=jnp.float32) * 2.0 / rot))
    ang = pos.astype(jnp.float32)[..., None] * inv_freq
    cos = jnp.cos(ang)[:, :, None, :]
    sin = jnp.sin(ang)[:, :, None, :]
    xr = x[..., :rot].astype(jnp.float32)
    x1, x2 = xr[..., :half], xr[..., half:]
    rotated = jnp.concatenate([x1 * cos - x2 * sin, x2 * cos + x1 * sin], axis=-1)
    return jnp.concatenate([rotated.astype(x.dtype), x[..., rot:]], axis=-1)


def causal_dwconv(x, w, b):
    width = w.shape[0]
    y = lax.conv_general_dilated(
        x, w[:, None, :].astype(x.dtype), window_strides=(1,), padding=[(width - 1, 0)],
        dimension_numbers=('NWC', 'WIO', 'NWC'), feature_group_count=x.shape[-1])
    return y + b


def ssd_mixer(z, xbc, dt, conv_w, conv_b, dt_bias, a_log, d_skip, norm_g):
    bsz, L, _ = xbc.shape
    G, R, P, N, Q = SSD_GROUPS, SSD_HEADS // SSD_GROUPS, SSD_HEAD_DIM, SSD_STATE, SSD_CHUNK
    nc = L // Q
    xbc = jax.nn.silu(causal_dwconv(xbc, conv_w, conv_b))
    xs, bm, cm = jnp.split(xbc, [SSD_INNER, SSD_INNER + G * N], axis=-1)
    xs = xs.reshape(bsz, L, G, R, P)
    bm = bm.reshape(bsz, L, G, N)
    cm = cm.reshape(bsz, L, G, N)
    dt = jax.nn.softplus((dt + dt_bias).astype(jnp.float32)).reshape(bsz, L, G, R)
    a = -jnp.exp(a_log.astype(jnp.float32)).reshape(G, R)
    a_dt = dt * a
    xdt = xs.astype(jnp.float32) * dt[..., None]

    xc = xdt.reshape(bsz, nc, Q, G, R, P)
    bc = bm.reshape(bsz, nc, Q, G, N).astype(jnp.float32)
    cc = cm.reshape(bsz, nc, Q, G, N).astype(jnp.float32)
    ac = jnp.moveaxis(a_dt.reshape(bsz, nc, Q, G, R), 2, -1)
    cs = jnp.cumsum(ac, axis=-1)

    causal = jnp.tril(jnp.ones((Q, Q), dtype=bool))
    seg = cs[..., :, None] - cs[..., None, :]
    lmat = jnp.exp(jnp.where(causal, seg, -jnp.inf))
    cb = jnp.einsum('bclgn,bcsgn->bcgls', cc, bc)
    y_diag = jnp.einsum('bcgls,bcgrls,bcsgrp->bclgrp', cb, lmat, xc)

    decay = jnp.exp(cs[..., -1:] - cs)
    states = jnp.einsum('bclgn,bcgrl,bclgrp->bcgrpn', bc, decay, xc)
    chunk_decay = jnp.exp(cs[..., -1])

    def step(h, inp):
        st, dec = inp
        return h * dec[..., None, None] + st, h

    h0 = jnp.zeros((bsz, G, R, P, N), jnp.float32)
    _, prev = lax.scan(step, h0, (jnp.moveaxis(states, 1, 0), jnp.moveaxis(chunk_decay, 1, 0)))
    prev = jnp.moveaxis(prev, 0, 1)
    y_off = jnp.einsum('bclgn,bcgrpn,bcgrl->bclgrp', cc, prev, jnp.exp(cs))

    y = (y_diag + y_off).reshape(bsz, L, G, R, P)
    y = y + xs.astype(jnp.float32) * d_skip.astype(jnp.float32).reshape(G, R)[..., None]
    y = y.reshape(bsz, L, SSD_INNER).astype(z.dtype)
    yg = (y * jax.nn.silu(z)).reshape(bsz, L, G, SSD_INNER // G)
    return rmsnorm(yg, norm_g.reshape(G, SSD_INNER // G)).reshape(bsz, L, SSD_INNER)


def dsa_mixer(q, k, v, iq, ik, iw, pos, qn_g, kn_g):
    bsz, L, _ = q.shape
    q = rope_partial(rmsnorm(q.reshape(bsz, L, ATTN_HEADS, HEAD_DIM), qn_g), pos)
    k = rope_partial(rmsnorm(k.reshape(bsz, L, ATTN_HEADS, HEAD_DIM), kn_g), pos)
    v = v.reshape(bsz, L, ATTN_HEADS, HEAD_DIM)
    iq = rope_partial(iq.reshape(bsz, L, IDX_HEADS, IDX_DIM), pos)
    ik = rope_partial(ik[:, :, None, :], pos)[:, :, 0]
    iw = iw * ((IDX_HEADS ** -0.5) * (IDX_DIM ** -0.5))
    topk = min(TOPK_MAX, L // 4)
    n_blocks = L // Q_BLOCK
    key_pos = jnp.arange(L)
    scale = HEAD_DIM ** -0.5

    def block(i):
        q0 = i * Q_BLOCK
        qi = lax.dynamic_slice_in_dim(q, q0, Q_BLOCK, axis=1)
        iqi = lax.dynamic_slice_in_dim(iq, q0, Q_BLOCK, axis=1)
        iwi = lax.dynamic_slice_in_dim(iw, q0, Q_BLOCK, axis=1)
        qpos = q0 + jnp.arange(Q_BLOCK)
        rel = jax.nn.relu(jnp.einsum('bqhd,bsd->bqhs', iqi, ik).astype(jnp.float32))
        iscore = jnp.einsum('bqh,bqhs->bqs', iwi.astype(jnp.float32), rel)
        iscore = jnp.where(key_pos[None, None, :] <= qpos[None, :, None], iscore, -jnp.inf)
        _, sel = lax.top_k(iscore, topk)
        ksel = jax.vmap(lambda kk, ii: kk[ii])(k, sel)
        vsel = jax.vmap(lambda vv, ii: vv[ii])(v, sel)
        logits = jnp.einsum('bqhd,bqkhd->bhqk', qi, ksel).astype(jnp.float32) * scale
        valid = (sel <= qpos[None, :, None])[:, None]
        p = jax.nn.softmax(jnp.where(valid, logits, -jnp.inf), axis=-1)
        return jnp.einsum('bhqk,bqkhd->bqhd', p.astype(v.dtype), vsel)

    out = lax.map(block, jnp.arange(n_blocks))
    return jnp.moveaxis(out, 0, 1).reshape(bsz, L, ATTN_INNER)


def mem_mixer(mq, mem_kv, qn_g, kn_g):
    bsz, L, _ = mq.shape
    q = rmsnorm(mq.reshape(bsz, L, MEM_HEADS, HEAD_DIM), qn_g)
    mk, mv = jnp.split(mem_kv, 2, axis=-1)
    mk = rmsnorm(mk.reshape(bsz, -1, MEM_HEADS, HEAD_DIM), kn_g)
    mv = mv.reshape(bsz, -1, MEM_HEADS, HEAD_DIM)
    logits = jnp.einsum('bqhd,bmhd->bhqm', q, mk).astype(jnp.float32) * (HEAD_DIM ** -0.5)
    p = jax.nn.softmax(logits, axis=-1)
    return jnp.einsum('bhqm,bmhd->bqhd', p.astype(mv.dtype), mv).reshape(bsz, L, MEM_INNER)


def conv_ffn(h, w_up, conv_w, conv_b, w_down):
    u = causal_dwconv(h @ w_up, conv_w, conv_b)
    g, val = jnp.split(u, 2, axis=-1)
    return (jax.nn.silu(g) * val) @ w_down


def setup_inputs(seed: int = 0) -> dict:
    key = jax.random.key(seed)
    ks = jax.random.split(key, 32)
    f32 = jnp.float32

    def w(k, shape, fan_in):
        return jax.random.normal(k, shape, f32) * (fan_in ** -0.5)

    def gain(k, shape):
        return 1.0 + 0.02 * jax.random.normal(k, shape, f32)

    x = jax.random.normal(ks[0], (BATCH, SEQ, D_MODEL), f32)
    mem = jax.random.normal(ks[1], (BATCH, N_MEM, D_MODEL), f32)
    offset = jax.random.randint(ks[2], (BATCH, 1), 0, 1024, dtype=jnp.int32)
    pos = (offset + jnp.arange(SEQ, dtype=jnp.int32)[None, :]).astype(jnp.int32)

    dt0 = jnp.exp(jax.random.uniform(ks[6], (DEPTH, SSD_HEADS), f32)
                  * (jnp.log(0.1) - jnp.log(0.001)) + jnp.log(0.001))
    ssd_dt_bias = dt0 + jnp.log(-jnp.expm1(-dt0))
    ssd_a_log = jnp.log(jax.random.uniform(ks[7], (DEPTH, SSD_HEADS), f32, 1.0, 16.0))

    return {
        "x": x,
        "mem": mem,
        "pos": pos,
        "mix_norm_g": gain(ks[3], (DEPTH, D_MODEL)),
        "w_in": w(ks[4], (DEPTH, D_MODEL, N_IN), D_MODEL),
        "ssd_conv_w": 0.5 * jax.random.normal(ks[5], (DEPTH, SSD_CONV, SSD_CONV_DIM), f32),
        "ssd_conv_b": 0.01 * jax.random.normal(ks[8], (DEPTH, SSD_CONV_DIM), f32),
        "ssd_dt_bias": ssd_dt_bias,
        "ssd_a_log": ssd_a_log,
        "ssd_d": 1.0 + 0.1 * jax.random.normal(ks[9], (DEPTH, SSD_HEADS), f32),
        "ssd_norm_g": gain(ks[10], (DEPTH, SSD_INNER)),
        "attn_q_norm_g": gain(ks[11], (DEPTH, HEAD_DIM)),
        "attn_k_norm_g": gain(ks[12], (DEPTH, HEAD_DIM)),
        "mem_norm_g": gain(ks[13], (DEPTH, D_MODEL)),
        "w_mem_kv": w(ks[14], (DEPTH, D_MODEL, 2 * MEM_INNER), D_MODEL),
        "mem_q_norm_g": gain(ks[15], (DEPTH, HEAD_DIM)),
        "mem_k_norm_g": gain(ks[16], (DEPTH, HEAD_DIM)),
        "w_out": w(ks[17], (DEPTH, D_MIX, D_MODEL), D_MIX),
        "ffn_norm_g": gain(ks[18], (DEPTH, D_MODEL)),
        "w_up": w(ks[19], (DEPTH, D_MODEL, 2 * D_FF), D_MODEL),
        "ffn_conv_w": w(ks[20], (DEPTH, FFN_CONV, 2 * D_FF), FFN_CONV),
        "ffn_conv_b": 0.01 * jax.random.normal(ks[21], (DEPTH, 2 * D_FF), f32),
        "w_down": w(ks[22], (DEPTH, D_FF, D_MODEL), D_FF),
    }


def reference(x, mem, pos, mix_norm_g, w_in, ssd_conv_w, ssd_conv_b, ssd_dt_bias, ssd_a_log,
              ssd_d, ssd_norm_g, attn_q_norm_g, attn_k_norm_g, mem_norm_g, w_mem_kv,
              mem_q_norm_g, mem_k_norm_g, w_out, ffn_norm_g, w_up, ffn_conv_w, ffn_conv_b,
              w_down):
    for l in range(DEPTH):
        h = rmsnorm(x, mix_norm_g[l])
        proj = h @ w_in[l]
        z, xbc, dt, q, k, v, iq, ik, iw, mq = jnp.split(proj, IN_SPLITS, axis=-1)
        y_ssd = ssd_mixer(z, xbc, dt, ssd_conv_w[l], ssd_conv_b[l], ssd_dt_bias[l],
                          ssd_a_log[l], ssd_d[l], ssd_norm_g[l])
        y_dsa = dsa_mixer(q, k, v, iq, ik, iw, pos, attn_q_norm_g[l], attn_k_norm_g[l])
        mem_kv = rmsnorm(mem, mem_norm_g[l]) @ w_mem_kv[l]
        y_mem = mem_mixer(mq, mem_kv, mem_q_norm_g[l], mem_k_norm_g[l])
        y_mix = jnp.concatenate([y_ssd.astype(x.dtype), y_dsa.astype(x.dtype),
                                 y_mem.astype(x.dtype)], axis=-1)
        x = x + (y_mix @ w_out[l]).astype(x.dtype)
        x = x + conv_ffn(rmsnorm(x, ffn_norm_g[l]), w_up[l], ffn_conv_w[l], ffn_conv_b[l],
                         w_down[l]).astype(x.dtype)
    return x
```

```python
import functools

import numpy as np
import jax
import jax.numpy as jnp
from jax import lax
from jax.experimental import pallas as pl
from jax.experimental.pallas import tpu as pltpu

F32 = jnp.float32
BF16 = jnp.bfloat16
I32 = jnp.int32

EPS = 1e-6
ROPE_THETA = 500000.0
HEAD_DIM = 64
ROT_HALF = 8
SSD_HEADS = 8
SSD_INNER = 512
SSD_GROUPS = 2
SSD_STATE = 128
SSD_CONV = 4
SSD_CHUNK = 128
SSD_CONV_DIM = 1024
ATTN_INNER = 256
IDX_HEADS = 4
IDX_DIM = 64
TOPK_MAX = 256
Q_BLOCK = 128
N_MEM = 256
MEM_INNER = 256
D_FF = 2816
FFN_CONV = 3

SLAB_W = 2944
SMALL_DT = 64
SMALL_IW = 72
BLK_XBC, BLK_Z, BLK_Q, BLK_K, BLK_V, BLK_IQ, BLK_MQ, BLK_SMALL = 0, 2, 6, 7, 8, 9, 10, 22

KEY_CHUNK = 512
FF_TILE = 256
NEG_BIG = -1e30
INT_MIN = -2 ** 31
KEY_NEG_INF = int(np.array(0xFF800000 ^ 0x7FFFFFFF, dtype=np.uint32).astype(np.int32))

VMEM_LIMIT = 56 * 1024 * 1024


def _cparams(*sem):
    return pltpu.CompilerParams(dimension_semantics=sem, vmem_limit_bytes=VMEM_LIMIT)


def _nt(a, b):
    return lax.dot_general(a, b, (((1,), (1,)), ((), ())), preferred_element_type=F32)


def _mm(a, b):
    return jnp.dot(a, b, preferred_element_type=F32)


def _sigmoid(x):
    return 1.0 / (1.0 + jnp.exp(-x))


def _silu(x):
    return x * _sigmoid(x)


def _softplus(x):
    return jnp.maximum(x, 0.0) + jnp.log(1.0 + jnp.exp(-jnp.abs(x)))


def _rope_kernel(pos_ref, invf_ref, sgn_ref, cos_ref, sin_ref):
    ang = pos_ref[...].astype(F32) * invf_ref[...]
    cos_ref[...] = jnp.cos(ang)
    sin_ref[...] = jnp.sin(ang) * sgn_ref[...]


def _rope_tables(pos):
    T = pos.size
    half = ROT_HALF
    inv_freq = ROPE_THETA ** (-(jnp.arange(half, dtype=F32) * 2.0 / (2 * half)))
    lane = np.arange(128) % HEAD_DIM
    rot = lane < 2 * half
    invf = jnp.where(jnp.asarray(rot), inv_freq[jnp.asarray(lane % half)], 0.0).reshape(1, 128)
    sgn = jnp.asarray(np.where(lane < half, -1.0, np.where(rot, 1.0, 0.0)), F32).reshape(1, 128)
    posb = jnp.broadcast_to(pos.reshape(T, 1), (T, 128))
    tm = min(T, 2048)
    return pl.pallas_call(
        _rope_kernel,
        grid=(T // tm,),
        in_specs=[pl.BlockSpec((tm, 128), lambda i: (i, 0)),
                  pl.BlockSpec((1, 128), lambda i: (0, 0)),
                  pl.BlockSpec((1, 128), lambda i: (0, 0))],
        out_specs=[pl.BlockSpec((tm, 128), lambda i: (i, 0)),
                   pl.BlockSpec((tm, 128), lambda i: (i, 0))],
        out_shape=[jax.ShapeDtypeStruct((T, 128), F32)] * 2,
        compiler_params=_cparams("parallel"),
        name="rope_tables",
    )(posb, invf, sgn)


def _rope(x, cos, sin_s):
    w = x.shape[-1]
    if w == 256:
        cos = jnp.concatenate([cos, cos], axis=1)
        sin_s = jnp.concatenate([sin_s, sin_s], axis=1)
    lane = lax.broadcasted_iota(I32, x.shape, 1)
    first = (lane & (HEAD_DIM - 1)) < ROT_HALF
    partner = jnp.where(first, pltpu.roll(x, w - ROT_HALF, 1), pltpu.roll(x, ROT_HALF, 1))
    return x * cos + partner * sin_s


def _seg_ones(n):
    r = lax.broadcasted_iota(I32, (n, n), 0) // HEAD_DIM
    c = lax.broadcasted_iota(I32, (n, n), 1) // HEAD_DIM
    return jnp.where(r == c, 1.0, 0.0).astype(BF16)


def _head_rms(x, seg):
    sq = x * x
    hi = sq.astype(BF16)
    lo = (sq - hi.astype(F32)).astype(BF16)
    ms = (_mm(hi, seg) + _mm(lo, seg)) * (1.0 / HEAD_DIM)
    return x * lax.rsqrt(ms + EPS)


def _in_proj_kernel(x_ref, g_ref, w_ref, o_ref):
    x = x_ref[...]
    r = lax.rsqrt(jnp.mean(x * x, axis=-1, keepdims=True) + EPS)
    h = ((x * r) * g_ref[...]).astype(BF16)
    o_ref[...] = _mm(h, w_ref[...])


def _in_proj(xf, g, w_slab):
    T, D = xf.shape
    tm = min(T, 512)
    return pl.pallas_call(
        _in_proj_kernel,
        grid=(T // tm,),
        in_specs=[pl.BlockSpec((tm, D), lambda i: (i, 0)),
                  pl.BlockSpec((1, D), lambda i: (0, 0)),
                  pl.BlockSpec((D, SLAB_W), lambda i: (0, 0))],
        out_specs=pl.BlockSpec((tm, SLAB_W), lambda i: (i, 0)),
        out_shape=jax.ShapeDtypeStruct((T, SLAB_W), F32),
        compiler_params=_cparams("parallel"),
        name="in_proj",
    )(xf, g.reshape(1, D), w_slab)


def _slab_weight(w_in):
    d = w_in.shape[0]
    cols = [w_in[:, 512:1536], w_in[:, 0:512], w_in[:, 1544:1800], w_in[:, 1800:2056],
            w_in[:, 2056:2312], w_in[:, 2312:2568], w_in[:, 2636:2892], w_in[:, 2568:2632],
            w_in[:, 1536:1544], w_in[:, 2632:2636], jnp.zeros((d, SLAB_W - 2892), w_in.dtype)]
    return jnp.concatenate(cols, axis=1).astype(BF16)


def _ssd_kernel(xbc_ref, z_ref, sm_ref, cw_ref, cb_ref, dtb_ref, alog_ref, dsk_ref, ng_ref,
                o_ref, st_ref, carry_ref, *, ts):
    q = SSD_CHUNK
    p = HEAD_DIM

    @pl.when(pl.program_id(1) == 0)
    def _():
        st_ref[...] = jnp.zeros_like(st_ref)
        carry_ref[...] = jnp.zeros_like(carry_ref)

    cw = cw_ref[...]
    cb = cb_ref[...]
    dtb = dtb_ref[...]
    a_row = -jnp.exp(alog_ref[...])
    dsk = dsk_ref[...]
    ng = ng_ref[...]
    row8 = lax.broadcasted_iota(I32, (8, SSD_CONV_DIM), 0)
    rowi = lax.broadcasted_iota(I32, (q, q), 0)
    coli = lax.broadcasted_iota(I32, (q, q), 1)
    tril = rowi >= coli

    def chunk(c, _):
        off = pl.multiple_of(c * q, q)
        cur = xbc_ref[pl.ds(off, q), :]
        poff = pl.multiple_of(jnp.maximum(off - 8, 0), 8)
        prev8 = jnp.where(c == 0, carry_ref[...], xbc_ref[pl.ds(poff, 8), :])
        acc = cur * cw[SSD_CONV - 1:SSD_CONV, :] + cb
        for k in range(1, SSD_CONV):
            rk = pltpu.roll(cur, k, 0)
            pk = pltpu.roll(prev8, k, 0)
            top = jnp.where(row8 < k, pk, rk[0:8, :])
            sh = jnp.concatenate([top, rk[8:, :]], axis=0)
            acc = acc + sh * cw[SSD_CONV - 1 - k:SSD_CONV - k, :]
        xc = _silu(acc)
        xs = xc[:, :SSD_INNER]

        sm = sm_ref[pl.ds(off, q), :]
        dtv = _softplus(sm + dtb)
        cs = dtv * a_row
        s = 1
        while s < q:
            cs = cs + jnp.where(rowi >= s, pltpu.roll(cs, s, 0), 0.0)
            s *= 2
        cst = cs.T

        ys = []
        for g in range(SSD_GROUPS):
            b0 = SSD_INNER + g * SSD_STATE
            c0 = SSD_INNER + SSD_GROUPS * SSD_STATE + g * SSD_STATE
            bm = xc[:, b0:b0 + SSD_STATE]
            cmb = xc[:, c0:c0 + SSD_STATE].astype(BF16)
            cbm = _nt(cmb, bm.astype(BF16))
            bmt = bm.T.astype(BF16)
            for r in range(SSD_HEADS // SSD_GROUPS):
                h = g * (SSD_HEADS // SSD_GROUPS) + r
                hl = SMALL_DT + h
                colb = jnp.broadcast_to(cs[:, hl:hl + 1], (q, q))
                seg = colb - cst[hl:hl + 1, :]
                lm = jnp.exp(jnp.where(tril, seg, -jnp.inf))
                xs_h = xs[:, h * p:(h + 1) * p]
                xdt = xs_h * jnp.broadcast_to(dtv[:, hl:hl + 1], (q, p))
                yd = _mm((cbm * lm).astype(BF16), xdt.astype(BF16))
                cb64 = colb[:, :p]
                last = jnp.broadcast_to(cb64[q - 1:q, :], (q, p))
                xw = (xdt * jnp.exp(last - cb64)).astype(BF16)
                st_new = _mm(bmt, xw)
                st_prev = st_ref[h]
                yo = _mm(cmb, st_prev.astype(BF16)) * jnp.exp(cb64)
                st_ref[h] = st_prev * jnp.exp(last) + st_new
                ys.append(yd + yo + xs_h * dsk[:, h * p:(h + 1) * p])
        y = jnp.concatenate(ys, axis=1)
        yg = y * _silu(z_ref[pl.ds(off, q), :])
        gw = SSD_INNER // SSD_GROUPS
        outs = []
        for g in range(SSD_GROUPS):
            v = yg[:, g * gw:(g + 1) * gw]
            rr = lax.rsqrt(jnp.mean(v * v, axis=-1, keepdims=True) + EPS)
            outs.append((v * rr) * ng[:, g * gw:(g + 1) * gw])
        o_ref[pl.ds(off, q), :] = jnp.concatenate(outs, axis=1).astype(o_ref.dtype)
        return 0

    lax.fori_loop(0, ts // q, chunk, 0)
    carry_ref[...] = xbc_ref[ts - 8:ts, :]


def _ssd(slab, B, L, conv_w, conv_b, dt_bias, a_log, d_skip, norm_g):
    T = B * L
    ts = min(L, 1024)
    nj = L // ts
    dtb = jnp.zeros((1, 128), F32).at[0, SMALL_DT:SMALL_DT + SSD_HEADS].set(dt_bias)
    alog = jnp.zeros((1, 128), F32).at[0, SMALL_DT:SMALL_DT + SSD_HEADS].set(a_log)
    dsk = jnp.repeat(d_skip, HEAD_DIM).reshape(1, SSD_INNER)
    row = lambda w: pl.BlockSpec((1, w), lambda b, j: (0, 0))
    return pl.pallas_call(
        functools.partial(_ssd_kernel, ts=ts),
        grid=(B, nj),
        in_specs=[pl.BlockSpec((ts, SSD_CONV_DIM), lambda b, j: (b * nj + j, BLK_XBC)),
                  pl.BlockSpec((ts, SSD_INNER), lambda b, j: (b * nj + j, BLK_Z)),
                  pl.BlockSpec((ts, 128), lambda b, j: (b * nj + j, BLK_SMALL)),
                  pl.BlockSpec((SSD_CONV, SSD_CONV_DIM), lambda b, j: (0, 0)),
                  row(SSD_CONV_DIM), row(128), row(128), row(SSD_INNER), row(SSD_INNER)],
        out_specs=pl.BlockSpec((ts, SSD_INNER), lambda b, j: (b * nj + j, 0)),
        out_shape=jax.ShapeDtypeStruct((T, SSD_INNER), BF16),
        scratch_shapes=[pltpu.VMEM((SSD_HEADS, SSD_STATE, HEAD_DIM), F32),
                        pltpu.VMEM((8, SSD_CONV_DIM), F32)],
        compiler_params=_cparams("arbitrary", "arbitrary"),
        name="ssd",
    )(slab, slab, slab, conv_w, conv_b.reshape(1, -1), dtb, alog, dsk, norm_g.reshape(1, -1))


def _prep_kernel(q_ref, k_ref, v_ref, iq_ref, mq_ref, sm_ref, cos_ref, sin_ref,
                 qg_ref, kg_ref, mg_ref,
                 qn_ref, kn_ref, vt_ref, iqr_ref, ika_ref, ikb_ref, iwt_ref, mqn_ref):
    cos = cos_ref[...]
    sin_s = sin_ref[...]
    seg = _seg_ones(ATTN_INNER)
    scale = HEAD_DIM ** -0.5
    qn = _rope(_head_rms(q_ref[...], seg) * qg_ref[...], cos, sin_s) * scale
    qn_ref[...] = qn.astype(BF16)
    kn = _rope(_head_rms(k_ref[...], seg) * kg_ref[...], cos, sin_s)
    kn_ref[...] = kn.astype(BF16)
    vt_ref[0] = v_ref[...].T.astype(BF16)
    iqr_ref[...] = _rope(iq_ref[...], cos, sin_s).astype(BF16)
    mqn_ref[...] = ((_head_rms(mq_ref[...], seg) * mg_ref[...]) * scale).astype(BF16)
    sm = sm_ref[...]
    ikr = _rope(sm, cos, sin_s)
    lane = lax.broadcasted_iota(I32, ikr.shape, 1)
    ika = jnp.where(lane < IDX_DIM, ikr, 0.0)
    ika_ref[...] = ika.astype(BF16)
    ikb_ref[...] = pltpu.roll(ika, IDX_DIM, 1).astype(BF16)
    smt = sm.T
    iwt_ref[...] = smt[SMALL_IW:SMALL_IW + 8, :] * ((IDX_HEADS ** -0.5) * (IDX_DIM ** -0.5))


def _prep(slab, cos_t, sin_t, qg, kg, mg):
    T = slab.shape[0]
    tm = KEY_CHUNK
    tile4 = lambda g: jnp.tile(g, ATTN_INNER // HEAD_DIM).reshape(1, ATTN_INNER)
    col = lambda w, blk: pl.BlockSpec((tm, w), lambda i: (i, blk))
    row = lambda w: pl.BlockSpec((1, w), lambda i: (0, 0))
    out = lambda w: pl.BlockSpec((tm, w), lambda i: (i, 0))
    return pl.pallas_call(
        _prep_kernel,
        grid=(T // tm,),
        in_specs=[col(256, BLK_Q), col(256, BLK_K), col(256, BLK_V), col(256, BLK_IQ),
                  col(256, BLK_MQ), col(128, BLK_SMALL), out(128), out(128),
                  row(256), row(256), row(256)],
        out_specs=[out(256), out(256), pl.BlockSpec((1, 256, tm), lambda i: (i, 0, 0)),
                   out(256), out(128), out(128), pl.BlockSpec((8, tm), lambda i: (0, i)), out(256)],
        out_shape=[jax.ShapeDtypeStruct((T, 256), BF16), jax.ShapeDtypeStruct((T, 256), BF16),
                   jax.ShapeDtypeStruct((T // tm, 256, tm), BF16),
                   jax.ShapeDtypeStruct((T, 256), BF16), jax.ShapeDtypeStruct((T, 128), BF16),
                   jax.ShapeDtypeStruct((T, 128), BF16), jax.ShapeDtypeStruct((8, T), F32),
                   jax.ShapeDtypeStruct((T, 256), BF16)],
        compiler_params=_cparams("parallel"),
        name="attn_prep",
    )(slab, slab, slab, slab, slab, slab, cos_t, sin_t, tile4(qg), tile4(kg), tile4(mg))


def _mem_kv_kernel(mem_ref, g_ref, w_ref, kg_ref, mk_ref, mvt_ref):
    x = mem_ref[...]
    r = lax.rsqrt(jnp.mean(x * x, axis=-1, keepdims=True) + EPS)
    h = ((x * r) * g_ref[...]).astype(BF16)
    kv = _mm(h, w_ref[...])
    seg = _seg_ones(MEM_INNER)
    mk_ref[...] = (_head_rms(kv[:, :MEM_INNER], seg) * kg_ref[...]).astype(BF16)
    mvt_ref[0] = kv[:, MEM_INNER:].T.astype(BF16)


def _mem_kv(mem, g, w, kg):
    B, M, D = mem.shape
    return pl.pallas_call(
        _mem_kv_kernel,
        grid=(B,),
        in_specs=[pl.BlockSpec((M, D), lambda b: (b, 0)),
                  pl.BlockSpec((1, D), lambda b: (0, 0)),
                  pl.BlockSpec((D, 2 * MEM_INNER), lambda b: (0, 0)),
                  pl.BlockSpec((1, MEM_INNER), lambda b: (0, 0))],
        out_specs=[pl.BlockSpec((M, MEM_INNER), lambda b: (b, 0)),
                   pl.BlockSpec((1, MEM_INNER, M), lambda b: (b, 0, 0))],
        out_shape=[jax.ShapeDtypeStruct((B * M, MEM_INNER), BF16),
                   jax.ShapeDtypeStruct((B, MEM_INNER, M), BF16)],
        compiler_params=_cparams("parallel"),
        name="mem_kv",
    )(mem.reshape(B * M, D), g.reshape(1, D), w.astype(BF16),
      jnp.tile(kg, MEM_INNER // HEAD_DIM).reshape(1, MEM_INNER))


def _split_heads(x):
    lane = lax.broadcasted_iota(I32, (Q_BLOCK, 128), 1)
    lo = lane < HEAD_DIM
    zero = jnp.zeros((Q_BLOCK, 128), x.dtype)
    pairs = []
    for pr in range(2):
        xp = x[:, pr * 128:(pr + 1) * 128]
        pairs.append(jnp.concatenate([jnp.where(lo, xp, zero), jnp.where(lo, zero, xp)], axis=0))
    return pairs


def _attn_kernel(qn_ref, iqr_ref, iwt_ref, mqn_ref, ika_ref, ikb_ref, kn_ref, vt_ref, mk_ref, mvt_ref,
                 o_ref, keys_ref, *, topk):
    kc = KEY_CHUNK
    qb = Q_BLOCK
    i = pl.program_id(1)
    q0 = i * qb
    nch = (q0 + qb + kc - 1) // kc
    qpos = q0 + lax.broadcasted_iota(I32, (1, qb), 1)
    iw = iwt_ref[...]
    iq = iqr_ref[...]
    iq_st = jnp.concatenate([iq[:, :128], iq[:, 128:]], axis=0)
    rows = lax.broadcasted_iota(I32, (kc, qb), 0)

    def p1(c, _):
        off = pl.multiple_of(c * kc, kc)
        ra = _nt(ika_ref[pl.ds(off, kc), :], iq_st)
        rb = _nt(ikb_ref[pl.ds(off, kc), :], iq_st)
        isc = (iw[0:1, :] * jnp.maximum(ra[:, :qb], 0.0) + iw[1:2, :] * jnp.maximum(rb[:, :qb], 0.0)
               + iw[2:3, :] * jnp.maximum(ra[:, qb:], 0.0) + iw[3:4, :] * jnp.maximum(rb[:, qb:], 0.0))
        isc = jnp.where(rows + off <= qpos, isc, -jnp.inf)
        bits = pltpu.bitcast(isc, I32)
        keys_ref[pl.ds(off, kc), :] = jnp.where(bits < 0, bits ^ 0x7FFFFFFF, bits)
        return 0

    lax.fori_loop(0, nch, p1, 0)

    def count_ge(cand):
        def body(c, acc):
            off = pl.multiple_of(c * kc, kc)
            k = keys_ref[pl.ds(off, kc), :]
            m = jnp.where(k >= cand, 1, 0).astype(I32)
            return acc + jnp.sum(m.reshape(kc // 8, 8, qb), axis=0)
        acc = lax.fori_loop(0, nch, body, jnp.zeros((8, qb), I32))
        return jnp.sum(acc, axis=0, keepdims=True)

    def bis(it, t_u):
        cand_u = t_u | jnp.left_shift(jnp.int32(1), 31 - it)
        cnt = count_ge(cand_u ^ INT_MIN)
        return jnp.where(cnt >= topk, cand_u, t_u)

    t = lax.fori_loop(0, 32, bis, jnp.zeros((1, qb), I32)) ^ INT_MIN
    c_gt = count_ge(t + 1)
    r_eq = jnp.where(t == KEY_NEG_INF, 0, topk - c_gt).astype(F32)

    q_pairs = _split_heads(qn_ref[...])
    tri = jnp.where(lax.broadcasted_iota(I32, (kc, kc), 0) >= lax.broadcasted_iota(I32, (kc, kc), 1),
                    1.0, 0.0).astype(BF16)
    nh = ATTN_INNER // HEAD_DIM

    def p3(c, carry):
        eq_cnt, ms, ls, accs = carry
        off = pl.multiple_of(c * kc, kc)
        k = keys_ref[pl.ds(off, kc), :]
        eq = k == t
        pref = _mm(tri, jnp.where(eq, 1.0, 0.0).astype(BF16)) + eq_cnt
        sel = (k > t) | (eq & (pref <= r_eq))
        kn = kn_ref[pl.ds(off, kc), :]
        s_pairs = [_nt(kn[:, :128], q_pairs[0]), _nt(kn[:, 128:], q_pairs[1])]
        vt = vt_ref[c]
        ms2, ls2, accs2 = [], [], []
        for h in range(nh):
            s = s_pairs[h // 2][:, (h % 2) * qb:(h % 2 + 1) * qb]
            s = jnp.where(sel, s, -jnp.inf)
            cm = jnp.max(jnp.max(s.reshape(kc // 8, 8, qb), axis=0), axis=0, keepdims=True)
            m_new = jnp.maximum(ms[h], cm)
            alpha = jnp.exp(ms[h] - m_new)
            pexp = jnp.exp(s - m_new)
            psum = jnp.sum(jnp.sum(pexp.reshape(kc // 8, 8, qb), axis=0), axis=0, keepdims=True)
            ls2.append(ls[h] * alpha + psum)
            accs2.append(accs[h] * alpha + _mm(vt[h * HEAD_DIM:(h + 1) * HEAD_DIM, :], pexp.astype(BF16)))
            ms2.append(m_new)
        return pref[kc - 1:kc, :], tuple(ms2), tuple(ls2), tuple(accs2)

    init = (jnp.zeros((1, qb), F32),
            tuple(jnp.full((1, qb), NEG_BIG, F32) for _ in range(nh)),
            tuple(jnp.zeros((1, qb), F32) for _ in range(nh)),
            tuple(jnp.zeros((HEAD_DIM, qb), F32) for _ in range(nh)))
    _, _, ls, accs = lax.fori_loop(0, nch, p3, init)
    outs = [accs[h] * (1.0 / ls[h]) for h in range(nh)]

    m_pairs = _split_heads(mqn_ref[...])
    mk = mk_ref[...]
    mvt = mvt_ref[0]
    for pr in range(2):
        s2 = _nt(mk[:, pr * 128:(pr + 1) * 128], m_pairs[pr])
        for hh in range(2):
            h = pr * 2 + hh
            s = s2[:, hh * qb:(hh + 1) * qb]
            mx = jnp.max(s, axis=0, keepdims=True)
            pexp = jnp.exp(s - mx)
            den = jnp.sum(pexp, axis=0, keepdims=True)
            o = _mm(mvt[h * HEAD_DIM:(h + 1) * HEAD_DIM, :], pexp.astype(BF16))
            outs.append(o * (1.0 / den))
    o_ref[...] = jnp.concatenate(outs, axis=0).T.astype(o_ref.dtype)


def _attention(qn, iqr, iwt, mqn, ika, ikb, kn, vt, mk, mvt, B, L):
    T = B * L
    nq = L // Q_BLOCK
    ncl = L // KEY_CHUNK
    topk = min(TOPK_MAX, L // 4)
    qblk = lambda w: pl.BlockSpec((Q_BLOCK, w), lambda b, i: (b * nq + i, 0))
    kblk = lambda w: pl.BlockSpec((L, w), lambda b, i: (b, 0))
    return pl.pallas_call(
        functools.partial(_attn_kernel, topk=topk),
        grid=(B, nq),
        in_specs=[qblk(256), qblk(256), pl.BlockSpec((8, Q_BLOCK), lambda b, i: (0, b * nq + i)), qblk(256),
                  kblk(128), kblk(128), kblk(256),
                  pl.BlockSpec((ncl, 256, KEY_CHUNK), lambda b, i: (b, 0, 0)),
                  pl.BlockSpec((N_MEM, MEM_INNER), lambda b, i: (b, 0)),
                  pl.BlockSpec((1, MEM_INNER, N_MEM), lambda b, i: (b, 0, 0))],
        out_specs=pl.BlockSpec((Q_BLOCK, ATTN_INNER + MEM_INNER), lambda b, i: (b * nq + i, 0)),
        out_shape=jax.ShapeDtypeStruct((T, ATTN_INNER + MEM_INNER), BF16),
        scratch_shapes=[pltpu.VMEM((L, Q_BLOCK), I32)],
        compiler_params=_cparams("parallel", "arbitrary"),
        name="attention",
    )(qn, iqr, iwt, mqn, ika, ikb, kn, vt, mk, mvt)


def _out_proj_kernel(x_ref, ys_ref, ya_ref, w_ref, g_ref, x1_ref, hn_ref):
    w = w_ref[...]
    x1 = x_ref[...] + _mm(ys_ref[...], w[:SSD_INNER, :]) + _mm(ya_ref[...], w[SSD_INNER:, :])
    x1_ref[...] = x1
    r = lax.rsqrt(jnp.mean(x1 * x1, axis=-1, keepdims=True) + EPS)
    hn_ref[...] = ((x1 * r) * g_ref[...]).astype(BF16)


def _out_proj(xf, y_ssd, y_att, w_out, g):
    T, D = xf.shape
    tm = min(T, 512)
    blk = lambda w: pl.BlockSpec((tm, w), lambda i: (i, 0))
    return pl.pallas_call(
        _out_proj_kernel,
        grid=(T // tm,),
        in_specs=[blk(D), blk(SSD_INNER), blk(ATTN_INNER + MEM_INNER),
                  pl.BlockSpec((D, D), lambda i: (0, 0)), pl.BlockSpec((1, D), lambda i: (0, 0))],
        out_specs=[blk(D), blk(D)],
        out_shape=[jax.ShapeDtypeStruct((T, D), F32), jax.ShapeDtypeStruct((T, D), BF16)],
        compiler_params=_cparams("parallel"),
        name="out_proj",
    )(xf, y_ssd, y_att, w_out.astype(BF16), g.reshape(1, D))


def _ffn_kernel(h_ref, halo_ref, wg_ref, wv_ref, cwg_ref, cwv_ref, cbg_ref, cbv_ref, wd_ref, x1_ref,
                o_ref, acc_ref, *, tiles_per_seq):
    i = pl.program_id(0)
    j = pl.program_id(1)
    h = h_ref[...]
    halo = halo_ref[...]
    halo = jnp.where(i % tiles_per_seq == 0, jnp.zeros_like(halo), halo)
    tm = h.shape[0]
    row8 = lax.broadcasted_iota(I32, (8, FF_TILE), 0)

    def conv(w_ref, cw_ref, cb_ref):
        w = w_ref[0]
        u = _mm(h, w)
        uh = _mm(halo, w)
        cw = cw_ref[...]
        acc = u * cw[FFN_CONV - 1:FFN_CONV, :] + cb_ref[...]
        for k in range(1, FFN_CONV):
            rk = pltpu.roll(u, k, 0)
            top = jnp.where(row8 < k, pltpu.roll(uh, k, 0), rk[0:8, :])
            sh = jnp.concatenate([top, rk[8:, :]], axis=0)
            acc = acc + sh * cw[FFN_CONV - 1 - k:FFN_CONV - k, :]
        return acc

    gate = conv(wg_ref, cwg_ref, cbg_ref)
    val = conv(wv_ref, cwv_ref, cbv_ref)
    a = (_silu(gate) * val).astype(BF16)
    d = _mm(a, wd_ref[...])

    @pl.when(j == 0)
    def _():
        acc_ref[...] = d

    @pl.when(j > 0)
    def _():
        acc_ref[...] += d

    @pl.when(j == pl.num_programs(1) - 1)
    def _():
        o_ref[...] = x1_ref[...] + acc_ref[...]


def _ffn(x1, hn, w_up, conv_w, conv_b, w_down, L):
    T, D = x1.shape
    tm = min(L, 1024)
    nf = D_FF // FF_TILE
    w3 = w_up.astype(BF16).reshape(D, 2 * nf, FF_TILE).transpose(1, 0, 2)
    cb = conv_b.reshape(1, 2 * D_FF)
    hb = tm // 8
    return pl.pallas_call(
        functools.partial(_ffn_kernel, tiles_per_seq=L // tm),
        grid=(T // tm, nf),
        in_specs=[pl.BlockSpec((tm, D), lambda i, j: (i, 0)),
                  pl.BlockSpec((8, D), lambda i, j: (jnp.maximum(i * hb - 1, 0), 0)),
                  pl.BlockSpec((1, D, FF_TILE), lambda i, j: (j, 0, 0)),
                  pl.BlockSpec((1, D, FF_TILE), lambda i, j: (nf + j, 0, 0)),
                  pl.BlockSpec((FFN_CONV, FF_TILE), lambda i, j: (0, j)),
                  pl.BlockSpec((FFN_CONV, FF_TILE), lambda i, j: (0, nf + j)),
                  pl.BlockSpec((1, FF_TILE), lambda i, j: (0, j)),
                  pl.BlockSpec((1, FF_TILE), lambda i, j: (0, nf + j)),
                  pl.BlockSpec((FF_TILE, D), lambda i, j: (j, 0)),
                  pl.BlockSpec((tm, D), lambda i, j: (i, 0))],
        out_specs=pl.BlockSpec((tm, D), lambda i, j: (i, 0)),
        out_shape=jax.ShapeDtypeStruct((T, D), F32),
        scratch_shapes=[pltpu.VMEM((tm, D), F32)],
        compiler_params=_cparams("parallel", "arbitrary"),
        name="conv_ffn",
    )(hn, hn, w3, w3, conv_w, conv_w, cb, cb, w_down.astype(BF16), x1)


def kernel(x, mem, pos, mix_norm_g, w_in, ssd_conv_w, ssd_conv_b, ssd_dt_bias, ssd_a_log, ssd_d,
           ssd_norm_g, attn_q_norm_g, attn_k_norm_g, mem_norm_g, w_mem_kv, mem_q_norm_g,
           mem_k_norm_g, w_out, ffn_norm_g, w_up, ffn_conv_w, ffn_conv_b, w_down):
    B, L, D = x.shape
    T = B * L
    depth = w_in.shape[0]
    assert L % max(KEY_CHUNK, 1024 if L >= 1024 else KEY_CHUNK) == 0 and D == 1024
    xf = x.reshape(T, D)
    cos_t, sin_t = _rope_tables(pos)
    for l in range(depth):
        slab = _in_proj(xf, mix_norm_g[l], _slab_weight(w_in[l]))
        y_ssd = _ssd(slab, B, L, ssd_conv_w[l], ssd_conv_b[l], ssd_dt_bias[l], ssd_a_log[l],
                     ssd_d[l], ssd_norm_g[l])
        qn, kn, vt, iqr, ika, ikb, iwt, mqn = _prep(slab, cos_t, sin_t, attn_q_norm_g[l],
                                                    attn_k_norm_g[l], mem_q_norm_g[l])
        mk, mvt = _mem_kv(mem, mem_norm_g[l], w_mem_kv[l], mem_k_norm_g[l])
        y_att = _attention(qn, iqr, iwt, mqn, ika, ikb, kn, vt, mk, mvt, B, L)
        x1, hn = _out_proj(xf, y_ssd, y_att, w_out[l], ffn_norm_g[l])
        xf = _ffn(x1, hn, w_up[l], ffn_conv_w[l], ffn_conv_b[l], w_down[l], L)
    return xf.reshape(B, L, D)
```

```python
import functools

import numpy as np
import jax
import jax.numpy as jnp
from jax import lax
from jax.experimental import pallas as pl
from jax.experimental.pallas import tpu as pltpu

F32 = jnp.float32
BF16 = jnp.bfloat16
I32 = jnp.int32
I16 = jnp.int16

EPS = 1e-6
ROPE_THETA = 500000.0
HEAD_DIM = 64
ROT_HALF = 8
SSD_HEADS = 8
SSD_INNER = 512
SSD_GROUPS = 2
SSD_STATE = 128
SSD_CONV = 4
SSD_CHUNK = 128
SSD_CONV_DIM = 1024
ATTN_INNER = 256
IDX_HEADS = 4
IDX_DIM = 64
TOPK_MAX = 256
Q_BLOCK = 256
N_MEM = 256
MEM_INNER = 256
D_FF = 2816
FFN_CONV = 3

SLAB_W = 2944
SMALL_DT = 64
SMALL_IW = 72
BLK_XBC, BLK_Z, BLK_Q, BLK_K, BLK_V, BLK_IQ, BLK_MQ, BLK_SMALL = 0, 2, 6, 7, 8, 9, 10, 22

KEY_CHUNK = 512
TIE_BLOCK = 256
FF_TILE = 256
FFN_ROWS = 64
NEG_BIG = -1e30
INT_MIN = -2 ** 31
KEY_NEG_INF = int(np.array(0xFF800000 ^ 0x7FFFFFFF, dtype=np.uint32).astype(np.int32))

VMEM_LIMIT = 56 * 1024 * 1024


def _cparams(*sem):
    return pltpu.CompilerParams(dimension_semantics=sem, vmem_limit_bytes=VMEM_LIMIT)


def _nt(a, b):
    return lax.dot_general(a, b, (((1,), (1,)), ((), ())), preferred_element_type=F32)


def _mm(a, b):
    return jnp.dot(a, b, preferred_element_type=F32)


def _sigmoid(x):
    return 1.0 / (1.0 + jnp.exp(-x))


def _silu(x):
    return x * _sigmoid(x)


def _softplus(x):
    return jnp.maximum(x, 0.0) + jnp.log(1.0 + jnp.exp(-jnp.abs(x)))


def _rope_kernel(pos_ref, invf_ref, sgn_ref, cos_ref, sin_ref):
    ang = pos_ref[...].astype(F32) * invf_ref[...]
    cos_ref[...] = jnp.cos(ang)
    sin_ref[...] = jnp.sin(ang) * sgn_ref[...]


def _rope_tables(pos):
    T = pos.size
    half = ROT_HALF
    inv_freq = ROPE_THETA ** (-(jnp.arange(half, dtype=F32) * 2.0 / (2 * half)))
    lane = np.arange(128) % HEAD_DIM
    rot = lane < 2 * half
    invf = jnp.where(jnp.asarray(rot), inv_freq[jnp.asarray(lane % half)], 0.0).reshape(1, 128)
    sgn = jnp.asarray(np.where(lane < half, -1.0, np.where(rot, 1.0, 0.0)), F32).reshape(1, 128)
    posb = jnp.broadcast_to(pos.reshape(T, 1), (T, 128))
    tm = min(T, 2048)
    return pl.pallas_call(
        _rope_kernel,
        grid=(T // tm,),
        in_specs=[pl.BlockSpec((tm, 128), lambda i: (i, 0)),
                  pl.BlockSpec((1, 128), lambda i: (0, 0)),
                  pl.BlockSpec((1, 128), lambda i: (0, 0))],
        out_specs=[pl.BlockSpec((tm, 128), lambda i: (i, 0)),
                   pl.BlockSpec((tm, 128), lambda i: (i, 0))],
        out_shape=[jax.ShapeDtypeStruct((T, 128), F32)] * 2,
        compiler_params=_cparams("parallel"),
        name="rope_tables",
    )(posb, invf, sgn)


def _rope(x, cos, sin_s):
    w = x.shape[-1]
    if w == 256:
        cos = jnp.concatenate([cos, cos], axis=1)
        sin_s = jnp.concatenate([sin_s, sin_s], axis=1)
    lane = lax.broadcasted_iota(I32, x.shape, 1)
    first = (lane & (HEAD_DIM - 1)) < ROT_HALF
    partner = jnp.where(first, pltpu.roll(x, w - ROT_HALF, 1), pltpu.roll(x, ROT_HALF, 1))
    return x * cos + partner * sin_s


def _seg_ones(n):
    r = lax.broadcasted_iota(I32, (n, n), 0) // HEAD_DIM
    c = lax.broadcasted_iota(I32, (n, n), 1) // HEAD_DIM
    return jnp.where(r == c, 1.0, 0.0).astype(BF16)


def _head_rms(x, seg):
    sq = x * x
    hi = sq.astype(BF16)
    lo = (sq - hi.astype(F32)).astype(BF16)
    ms = (_mm(hi, seg) + _mm(lo, seg)) * (1.0 / HEAD_DIM)
    return x * lax.rsqrt(ms + EPS)


def _in_proj_kernel(x_ref, g_ref, w_ref, o_ref):
    x = x_ref[...]
    r = lax.rsqrt(jnp.mean(x * x, axis=-1, keepdims=True) + EPS)
    h = ((x * r) * g_ref[...]).astype(BF16)
    o_ref[...] = _mm(h, w_ref[...])


def _in_proj(xf, g, w_slab):
    T, D = xf.shape
    tm = min(T, 512)
    return pl.pallas_call(
        _in_proj_kernel,
        grid=(T // tm,),
        in_specs=[pl.BlockSpec((tm, D), lambda i: (i, 0)),
                  pl.BlockSpec((1, D), lambda i: (0, 0)),
                  pl.BlockSpec((D, SLAB_W), lambda i: (0, 0))],
        out_specs=pl.BlockSpec((tm, SLAB_W), lambda i: (i, 0)),
        out_shape=jax.ShapeDtypeStruct((T, SLAB_W), F32),
        compiler_params=_cparams("parallel"),
        name="in_proj",
    )(xf, g.reshape(1, D), w_slab)


def _slab_weight(w_in):
    d = w_in.shape[0]
    cols = [w_in[:, 512:1536], w_in[:, 0:512], w_in[:, 1544:1800], w_in[:, 1800:2056],
            w_in[:, 2056:2312], w_in[:, 2312:2568], w_in[:, 2636:2892], w_in[:, 2568:2632],
            w_in[:, 1536:1544], w_in[:, 2632:2636], jnp.zeros((d, SLAB_W - 2892), w_in.dtype)]
    return jnp.concatenate(cols, axis=1).astype(BF16)


def _ssd_kernel(xbc_ref, z_ref, sm_ref, cw_ref, cb_ref, dtb_ref, alog_ref, dsk_ref, ng_ref,
                o_ref, st_ref, carry_ref, *, ts):
    q = SSD_CHUNK
    p = HEAD_DIM

    @pl.when(pl.program_id(1) == 0)
    def _():
        st_ref[...] = jnp.zeros_like(st_ref)
        carry_ref[...] = jnp.zeros_like(carry_ref)

    cw = cw_ref[...]
    cb = cb_ref[...]
    dtb = dtb_ref[...]
    a_row = -jnp.exp(alog_ref[...])
    dsk = dsk_ref[...]
    ng = ng_ref[...]
    row8 = lax.broadcasted_iota(I32, (8, SSD_CONV_DIM), 0)
    rowi = lax.broadcasted_iota(I32, (q, q), 0)
    coli = lax.broadcasted_iota(I32, (q, q), 1)
    tril = rowi >= coli

    def chunk(c, _):
        off = pl.multiple_of(c * q, q)
        cur = xbc_ref[pl.ds(off, q), :]
        poff = pl.multiple_of(jnp.maximum(off - 8, 0), 8)
        prev8 = jnp.where(c == 0, carry_ref[...], xbc_ref[pl.ds(poff, 8), :])
        acc = cur * cw[SSD_CONV - 1:SSD_CONV, :] + cb
        for k in range(1, SSD_CONV):
            rk = pltpu.roll(cur, k, 0)
            pk = pltpu.roll(prev8, k, 0)
            top = jnp.where(row8 < k, pk, rk[0:8, :])
            sh = jnp.concatenate([top, rk[8:, :]], axis=0)
            acc = acc + sh * cw[SSD_CONV - 1 - k:SSD_CONV - k, :]
        xc = _silu(acc)
        xs = xc[:, :SSD_INNER]

        sm = sm_ref[pl.ds(off, q), :]
        dtv = _softplus(sm + dtb)
        cs = dtv * a_row
        s = 1
        while s < q:
            cs = cs + jnp.where(rowi >= s, pltpu.roll(cs, s, 0), 0.0)
            s *= 2
        cst = cs.T

        ys = []
        for g in range(SSD_GROUPS):
            b0 = SSD_INNER + g * SSD_STATE
            c0 = SSD_INNER + SSD_GROUPS * SSD_STATE + g * SSD_STATE
            bm = xc[:, b0:b0 + SSD_STATE]
            cmb = xc[:, c0:c0 + SSD_STATE].astype(BF16)
            cbm = _nt(cmb, bm.astype(BF16))
            bmt = bm.T.astype(BF16)
            for r in range(SSD_HEADS // SSD_GROUPS):
                h = g * (SSD_HEADS // SSD_GROUPS) + r
                hl = SMALL_DT + h
                colb = jnp.broadcast_to(cs[:, hl:hl + 1], (q, q))
                seg = colb - cst[hl:hl + 1, :]
                lm = jnp.exp(jnp.where(tril, seg, -jnp.inf))
                xs_h = xs[:, h * p:(h + 1) * p]
                xdt = xs_h * jnp.broadcast_to(dtv[:, hl:hl + 1], (q, p))
                yd = _mm((cbm * lm).astype(BF16), xdt.astype(BF16))
                cb64 = colb[:, :p]
                last = jnp.broadcast_to(cb64[q - 1:q, :], (q, p))
                xw = (xdt * jnp.exp(last - cb64)).astype(BF16)
                st_new = _mm(bmt, xw)
                st_prev = st_ref[h]
                yo = _mm(cmb, st_prev.astype(BF16)) * jnp.exp(cb64)
                st_ref[h] = st_prev * jnp.exp(last) + st_new
                ys.append(yd + yo + xs_h * dsk[:, h * p:(h + 1) * p])
        y = jnp.concatenate(ys, axis=1)
        yg = y * _silu(z_ref[pl.ds(off, q), :])
        gw = SSD_INNER // SSD_GROUPS
        outs = []
        for g in range(SSD_GROUPS):
            v = yg[:, g * gw:(g + 1) * gw]
            rr = lax.rsqrt(jnp.mean(v * v, axis=-1, keepdims=True) + EPS)
            outs.append((v * rr) * ng[:, g * gw:(g + 1) * gw])
        o_ref[pl.ds(off, q), :] = jnp.concatenate(outs, axis=1).astype(o_ref.dtype)
        return 0

    lax.fori_loop(0, ts // q, chunk, 0)
    carry_ref[...] = xbc_ref[ts - 8:ts, :]


def _ssd(slab, B, L, conv_w, conv_b, dt_bias, a_log, d_skip, norm_g):
    T = B * L
    ts = min(L, 1024)
    nj = L // ts
    dtb = jnp.zeros((1, 128), F32).at[0, SMALL_DT:SMALL_DT + SSD_HEADS].set(dt_bias)
    alog = jnp.zeros((1, 128), F32).at[0, SMALL_DT:SMALL_DT + SSD_HEADS].set(a_log)
    dsk = jnp.repeat(d_skip, HEAD_DIM).reshape(1, SSD_INNER)
    row = lambda w: pl.BlockSpec((1, w), lambda b, j: (0, 0))
    return pl.pallas_call(
        functools.partial(_ssd_kernel, ts=ts),
        grid=(B, nj),
        in_specs=[pl.BlockSpec((ts, SSD_CONV_DIM), lambda b, j: (b * nj + j, BLK_XBC)),
                  pl.BlockSpec((ts, SSD_INNER), lambda b, j: (b * nj + j, BLK_Z)),
                  pl.BlockSpec((ts, 128), lambda b, j: (b * nj + j, BLK_SMALL)),
                  pl.BlockSpec((SSD_CONV, SSD_CONV_DIM), lambda b, j: (0, 0)),
                  row(SSD_CONV_DIM), row(128), row(128), row(SSD_INNER), row(SSD_INNER)],
        out_specs=pl.BlockSpec((ts, SSD_INNER), lambda b, j: (b * nj + j, 0)),
        out_shape=jax.ShapeDtypeStruct((T, SSD_INNER), BF16),
        scratch_shapes=[pltpu.VMEM((SSD_HEADS, SSD_STATE, HEAD_DIM), F32),
                        pltpu.VMEM((8, SSD_CONV_DIM), F32)],
        compiler_params=_cparams("arbitrary", "arbitrary"),
        name="ssd",
    )(slab, slab, slab, conv_w, conv_b.reshape(1, -1), dtb, alog, dsk, norm_g.reshape(1, -1))


def _prep_kernel(q_ref, k_ref, v_ref, iq_ref, mq_ref, sm_ref, cos_ref, sin_ref,
                 qg_ref, kg_ref, mg_ref,
                 qn_ref, kn_ref, vt_ref, iqr_ref, ika_ref, ikb_ref, iwt_ref, mqn_ref):
    cos = cos_ref[...]
    sin_s = sin_ref[...]
    seg = _seg_ones(ATTN_INNER)
    scale = HEAD_DIM ** -0.5
    qn = _rope(_head_rms(q_ref[...], seg) * qg_ref[...], cos, sin_s) * scale
    qn_ref[...] = qn.astype(BF16)
    kn = _rope(_head_rms(k_ref[...], seg) * kg_ref[...], cos, sin_s)
    kn_ref[...] = kn.astype(BF16)
    vt_ref[0] = v_ref[...].T.astype(BF16)
    iqr_ref[...] = _rope(iq_ref[...], cos, sin_s).astype(BF16)
    mqn_ref[...] = ((_head_rms(mq_ref[...], seg) * mg_ref[...]) * scale).astype(BF16)
    sm = sm_ref[...]
    ikr = _rope(sm, cos, sin_s)
    lane = lax.broadcasted_iota(I32, ikr.shape, 1)
    ika = jnp.where(lane < IDX_DIM, ikr, 0.0)
    ika_ref[...] = ika.astype(BF16)
    ikb_ref[...] = pltpu.roll(ika, IDX_DIM, 1).astype(BF16)
    smt = sm.T
    iwt_ref[...] = smt[SMALL_IW:SMALL_IW + 8, :] * ((IDX_HEADS ** -0.5) * (IDX_DIM ** -0.5))


def _prep(slab, cos_t, sin_t, qg, kg, mg):
    T = slab.shape[0]
    tm = KEY_CHUNK
    tile4 = lambda g: jnp.tile(g, ATTN_INNER // HEAD_DIM).reshape(1, ATTN_INNER)
    col = lambda w, blk: pl.BlockSpec((tm, w), lambda i: (i, blk))
    row = lambda w: pl.BlockSpec((1, w), lambda i: (0, 0))
    out = lambda w: pl.BlockSpec((tm, w), lambda i: (i, 0))
    return pl.pallas_call(
        _prep_kernel,
        grid=(T // tm,),
        in_specs=[col(256, BLK_Q), col(256, BLK_K), col(256, BLK_V), col(256, BLK_IQ),
                  col(256, BLK_MQ), col(128, BLK_SMALL), out(128), out(128),
                  row(256), row(256), row(256)],
        out_specs=[out(256), out(256), pl.BlockSpec((1, 256, tm), lambda i: (i, 0, 0)),
                   out(256), out(128), out(128), pl.BlockSpec((8, tm), lambda i: (0, i)), out(256)],
        out_shape=[jax.ShapeDtypeStruct((T, 256), BF16), jax.ShapeDtypeStruct((T, 256), BF16),
                   jax.ShapeDtypeStruct((T // tm, 256, tm), BF16),
                   jax.ShapeDtypeStruct((T, 256), BF16), jax.ShapeDtypeStruct((T, 128), BF16),
                   jax.ShapeDtypeStruct((T, 128), BF16), jax.ShapeDtypeStruct((8, T), F32),
                   jax.ShapeDtypeStruct((T, 256), BF16)],
        compiler_params=_cparams("parallel"),
        name="attn_prep",
    )(slab, slab, slab, slab, slab, slab, cos_t, sin_t, tile4(qg), tile4(kg), tile4(mg))


def _mem_kv_kernel(mem_ref, g_ref, w_ref, kg_ref, mk_ref, mvt_ref):
    x = mem_ref[...]
    r = lax.rsqrt(jnp.mean(x * x, axis=-1, keepdims=True) + EPS)
    h = ((x * r) * g_ref[...]).astype(BF16)
    kv = _mm(h, w_ref[...])
    seg = _seg_ones(MEM_INNER)
    mk_ref[...] = (_head_rms(kv[:, :MEM_INNER], seg) * kg_ref[...]).astype(BF16)
    mvt_ref[0] = kv[:, MEM_INNER:].T.astype(BF16)


def _mem_kv(mem, g, w, kg):
    B, M, D = mem.shape
    return pl.pallas_call(
        _mem_kv_kernel,
        grid=(B,),
        in_specs=[pl.BlockSpec((M, D), lambda b: (b, 0)),
                  pl.BlockSpec((1, D), lambda b: (0, 0)),
                  pl.BlockSpec((D, 2 * MEM_INNER), lambda b: (0, 0)),
                  pl.BlockSpec((1, MEM_INNER), lambda b: (0, 0))],
        out_specs=[pl.BlockSpec((M, MEM_INNER), lambda b: (b, 0)),
                   pl.BlockSpec((1, MEM_INNER, M), lambda b: (b, 0, 0))],
        out_shape=[jax.ShapeDtypeStruct((B * M, MEM_INNER), BF16),
                   jax.ShapeDtypeStruct((B, MEM_INNER, M), BF16)],
        compiler_params=_cparams("parallel"),
        name="mem_kv",
    )(mem.reshape(B * M, D), g.reshape(1, D), w.astype(BF16),
      jnp.tile(kg, MEM_INNER // HEAD_DIM).reshape(1, MEM_INNER))


def _split_heads(x):
    lane = lax.broadcasted_iota(I32, (Q_BLOCK, 128), 1)
    lo = lane < HEAD_DIM
    zero = jnp.zeros((Q_BLOCK, 128), x.dtype)
    heads = []
    for pr in range(2):
        xp = x[:, pr * 128:(pr + 1) * 128]
        heads += [jnp.where(lo, xp, zero), jnp.where(lo, zero, xp)]
    return heads


def _attn_kernel(qn_ref, iqr_ref, iwt_ref, mqn_ref, ika_ref, ikb_ref, kn_ref, vt_ref, mk_ref, mvt_ref,
                 o_ref, keys_ref, khi_ref, klo_ref, bias_ref, sm_ref, pb_ref, *, topk):
    kc = KEY_CHUNK
    qb = Q_BLOCK
    nh = ATTN_INNER // HEAD_DIM
    i = pl.program_id(1)
    q0 = i * qb
    nch = (q0 + qb + kc - 1) // kc
    iw = iwt_ref[...]
    iq = iqr_ref[...]
    rel = (lax.broadcasted_iota(I32, (kc, qb), 0) - lax.broadcasted_iota(I32, (kc, qb), 1)) - q0

    def chunk_off(c):
        return pl.multiple_of(c * kc, kc)

    def p1(c, _):
        off = chunk_off(c)
        ika = ika_ref[pl.ds(off, kc), :]
        ikb = ikb_ref[pl.ds(off, kc), :]
        isc = iw[0:1, :] * jnp.maximum(_nt(ika, iq[:, :128]), 0.0)
        isc = isc + iw[1:2, :] * jnp.maximum(_nt(ikb, iq[:, :128]), 0.0)
        isc = isc + iw[2:3, :] * jnp.maximum(_nt(ika, iq[:, 128:]), 0.0)
        isc = isc + iw[3:4, :] * jnp.maximum(_nt(ikb, iq[:, 128:]), 0.0)
        isc = jnp.where(rel <= -off, isc, -jnp.inf)
        bits = pltpu.bitcast(isc, I32)
        ks = jnp.where(bits < 0, bits ^ 0x7FFFFFFF, bits)
        keys_ref[pl.ds(off, kc), :] = ks
        khi_ref[pl.ds(off, kc), :] = (ks >> 16).astype(I16)
        klo_ref[pl.ds(off, kc), :] = ((ks & 0xFFFF) - 32768).astype(I16)
        return 0

    lax.fori_loop(0, nch, p1, 0)

    one_b = jnp.ones((kc, qb), BF16)
    zero_b = jnp.zeros((kc, qb), BF16)

    def count16(ref, cand, strict):
        def body(c, acc):
            kk = ref[pl.ds(chunk_off(c), kc), :]
            m = jnp.where(kk > cand if strict else kk >= cand, one_b, zero_b)
            m3 = m.reshape(kc // 16, 16, qb)
            parts = [m3[r] for r in range(kc // 16)]
            while len(parts) > 1:
                parts = [parts[r] + parts[r + 1] for r in range(0, len(parts), 2)]
            return acc + parts[0].astype(F32)
        acc = lax.fori_loop(0, nch, body, jnp.zeros((16, qb), F32))
        return jnp.sum(acc, axis=0, keepdims=True)

    def kth_largest16(ref, need):
        def bis(it, t_u):
            cand_u = t_u | jnp.left_shift(jnp.int32(1), 15 - it)
            cnt = count16(ref, (cand_u - 32768).astype(I16), False)
            return jnp.where(cnt >= need, cand_u, t_u)
        return lax.fori_loop(0, 16, bis, jnp.zeros((1, qb), I32)) - 32768

    t_hi = kth_largest16(khi_ref, float(topk))
    t_hi16 = t_hi.astype(I16)
    gt_hi = count16(khi_ref, t_hi16, True)

    def band(c, _):
        off = chunk_off(c)
        klo_ref[pl.ds(off, kc), :] = jnp.where(khi_ref[pl.ds(off, kc), :] == t_hi16,
                                               klo_ref[pl.ds(off, kc), :], jnp.int16(-32768))
        return 0

    lax.fori_loop(0, nch, band, 0)
    t_lo = kth_largest16(klo_ref, topk - gt_hi)
    gt_lo = count16(klo_ref, t_lo.astype(I16), True)
    t = t_hi * 65536 + (t_lo + 32768)
    r_eq = jnp.where(t == KEY_NEG_INF, 0.0, topk - (gt_hi + gt_lo))

    q_heads = _split_heads(qn_ref[...])
    tb = TIE_BLOCK
    tri = jnp.where(lax.broadcasted_iota(I32, (tb, tb), 0) >= lax.broadcasted_iota(I32, (tb, tb), 1),
                    1.0, 0.0).astype(BF16)

    def p3(c, carry):
        eq_cnt, ms, ls, accs = carry
        off = chunk_off(c)
        for r in range(kc // tb):
            k = keys_ref[pl.ds(off + r * tb, tb), :]
            eq = k == t
            pref = _mm(tri, jnp.where(eq, 1.0, 0.0).astype(BF16)) + eq_cnt
            eq_cnt = pref[tb - 1:tb, :]
            sel = (k > t) | (eq & (pref <= r_eq))
            bias_ref[r * tb:(r + 1) * tb, :] = jnp.where(sel, 0.0, -jnp.inf)
        kn = kn_ref[pl.ds(off, kc), :]
        m_news, alphas = [], []
        for h in range(nh):
            s = _nt(kn[:, (h // 2) * 128:(h // 2 + 1) * 128], q_heads[h]) + bias_ref[...]
            sm_ref[h] = s
            cm = jnp.max(jnp.max(s.reshape(kc // 8, 8, qb), axis=0), axis=0, keepdims=True)
            m_new = jnp.maximum(ms[h], cm)
            m_news.append(m_new)
            alphas.append(jnp.exp(ms[h] - m_new))
        ls2 = []
        for h in range(nh):
            pexp = jnp.exp(sm_ref[h] - m_news[h])
            psum = jnp.sum(jnp.sum(pexp.reshape(kc // 8, 8, qb), axis=0), axis=0, keepdims=True)
            ls2.append(ls[h] * alphas[h] + psum)
            pb_ref[h] = pexp.astype(BF16)
        vt = vt_ref[c]
        accs2 = [accs[h] * alphas[h] + _mm(vt[h * HEAD_DIM:(h + 1) * HEAD_DIM, :], pb_ref[h])
                 for h in range(nh)]
        return eq_cnt, tuple(m_news), tuple(ls2), tuple(accs2)

    init = (jnp.zeros((1, qb), F32),
            tuple(jnp.full((1, qb), NEG_BIG, F32) for _ in range(nh)),
            tuple(jnp.zeros((1, qb), F32) for _ in range(nh)),
            tuple(jnp.zeros((HEAD_DIM, qb), F32) for _ in range(nh)))
    _, _, ls, accs = lax.fori_loop(0, nch, p3, init)
    outs = [accs[h] * (1.0 / ls[h]) for h in range(nh)]

    m_heads = _split_heads(mqn_ref[...])
    mk = mk_ref[...]
    mvt = mvt_ref[0]
    for h in range(MEM_INNER // HEAD_DIM):
        s = _nt(mk[:, (h // 2) * 128:(h // 2 + 1) * 128], m_heads[h])
        mx = jnp.max(s, axis=0, keepdims=True)
        pexp = jnp.exp(s - mx)
        den = jnp.sum(pexp, axis=0, keepdims=True)
        o = _mm(mvt[h * HEAD_DIM:(h + 1) * HEAD_DIM, :], pexp.astype(BF16))
        outs.append(o * (1.0 / den))
    o_ref[...] = jnp.concatenate(outs, axis=0).T.astype(o_ref.dtype)


def _attention(qn, iqr, iwt, mqn, ika, ikb, kn, vt, mk, mvt, B, L):
    T = B * L
    nq = L // Q_BLOCK
    ncl = L // KEY_CHUNK
    topk = min(TOPK_MAX, L // 4)
    assert KEY_CHUNK >= topk and KEY_CHUNK % TIE_BLOCK == 0
    nh = ATTN_INNER // HEAD_DIM
    qblk = lambda w: pl.BlockSpec((Q_BLOCK, w), lambda b, i: (b * nq + i, 0))
    kblk = lambda w: pl.BlockSpec((L, w), lambda b, i: (b, 0))
    return pl.pallas_call(
        functools.partial(_attn_kernel, topk=topk),
        grid=(B, nq),
        in_specs=[qblk(256), qblk(256), pl.BlockSpec((8, Q_BLOCK), lambda b, i: (0, b * nq + i)), qblk(256),
                  kblk(128), kblk(128), kblk(256),
                  pl.BlockSpec((ncl, 256, KEY_CHUNK), lambda b, i: (b, 0, 0)),
                  pl.BlockSpec((N_MEM, MEM_INNER), lambda b, i: (b, 0)),
                  pl.BlockSpec((1, MEM_INNER, N_MEM), lambda b, i: (b, 0, 0))],
        out_specs=pl.BlockSpec((Q_BLOCK, ATTN_INNER + MEM_INNER), lambda b, i: (b * nq + i, 0)),
        out_shape=jax.ShapeDtypeStruct((T, ATTN_INNER + MEM_INNER), BF16),
        scratch_shapes=[pltpu.VMEM((L, Q_BLOCK), I32), pltpu.VMEM((L, Q_BLOCK), I16),
                        pltpu.VMEM((L, Q_BLOCK), I16), pltpu.VMEM((KEY_CHUNK, Q_BLOCK), F32),
                        pltpu.VMEM((nh, KEY_CHUNK, Q_BLOCK), F32),
                        pltpu.VMEM((nh, KEY_CHUNK, Q_BLOCK), BF16)],
        compiler_params=_cparams("parallel", "arbitrary"),
        name="attention",
    )(qn, iqr, iwt, mqn, ika, ikb, kn, vt, mk, mvt)


def _out_proj_kernel(x_ref, ys_ref, ya_ref, w_ref, g_ref, x1_ref, hn_ref):
    w = w_ref[...]
    x1 = x_ref[...] + _mm(ys_ref[...], w[:SSD_INNER, :]) + _mm(ya_ref[...], w[SSD_INNER:, :])
    x1_ref[...] = x1
    r = lax.rsqrt(jnp.mean(x1 * x1, axis=-1, keepdims=True) + EPS)
    hn_ref[...] = ((x1 * r) * g_ref[...]).astype(BF16)


def _out_proj(xf, y_ssd, y_att, w_out, g):
    T, D = xf.shape
    tm = min(T, 512)
    blk = lambda w: pl.BlockSpec((tm, w), lambda i: (i, 0))
    return pl.pallas_call(
        _out_proj_kernel,
        grid=(T // tm,),
        in_specs=[blk(D), blk(SSD_INNER), blk(ATTN_INNER + MEM_INNER),
                  pl.BlockSpec((D, D), lambda i: (0, 0)), pl.BlockSpec((1, D), lambda i: (0, 0))],
        out_specs=[blk(D), blk(D)],
        out_shape=[jax.ShapeDtypeStruct((T, D), F32), jax.ShapeDtypeStruct((T, D), BF16)],
        compiler_params=_cparams("parallel"),
        name="out_proj",
    )(xf, y_ssd, y_att, w_out.astype(BF16), g.reshape(1, D))


def _ffn_kernel(h_ref, halo_ref, wg_ref, wv_ref, cwg_ref, cwv_ref, cbg_ref, cbv_ref, wd_ref, x1_ref,
                o_ref, ug_ref, uv_ref, a_ref, acc_ref, *, tiles_per_seq):
    i = pl.program_id(0)
    j = pl.program_id(1)
    h = h_ref[...]
    halo = halo_ref[...]
    halo = jnp.where(i % tiles_per_seq == 0, jnp.zeros_like(halo), halo)
    tm = h.shape[0]
    rs = FFN_ROWS
    for w_ref, u_ref in ((wg_ref, ug_ref), (wv_ref, uv_ref)):
        w = w_ref[...]
        u_ref[0:8, :] = _mm(halo, w)
        u_ref[8:, :] = _mm(h, w)
    cwg = cwg_ref[...]
    cwv = cwv_ref[...]
    cbg = cbg_ref[...]
    cbv = cbv_ref[...]

    def conv(u_ref, r0, cw, cb):
        blk = u_ref[pl.ds(r0, rs + 8), :]
        acc = blk[8:, :] * cw[FFN_CONV - 1:FFN_CONV, :] + cb
        for k in range(1, FFN_CONV):
            acc = acc + blk[8 - k:8 - k + rs, :] * cw[FFN_CONV - 1 - k:FFN_CONV - k, :]
        return acc

    def rows(r, _):
        r0 = pl.multiple_of(r * rs, rs)
        gate = conv(ug_ref, r0, cwg, cbg)
        val = conv(uv_ref, r0, cwv, cbv)
        a_ref[pl.ds(r0, rs), :] = (_silu(gate) * val).astype(BF16)
        return 0

    lax.fori_loop(0, tm // rs, rows, 0, unroll=2)
    d = _mm(a_ref[...], wd_ref[...])

    @pl.when(j == 0)
    def _():
        acc_ref[...] = d

    @pl.when(j > 0)
    def _():
        acc_ref[...] += d

    @pl.when(j == pl.num_programs(1) - 1)
    def _():
        o_ref[...] = x1_ref[...] + acc_ref[...]


def _ffn(x1, hn, w_up, conv_w, conv_b, w_down, L):
    T, D = x1.shape
    tm = min(L, 1024)
    nf = D_FF // FF_TILE
    wb = w_up.astype(BF16)
    cb = conv_b.reshape(1, 2 * D_FF)
    hb = tm // 8
    return pl.pallas_call(
        functools.partial(_ffn_kernel, tiles_per_seq=L // tm),
        grid=(T // tm, nf),
        in_specs=[pl.BlockSpec((tm, D), lambda i, j: (i, 0)),
                  pl.BlockSpec((8, D), lambda i, j: (jnp.maximum(i * hb - 1, 0), 0)),
                  pl.BlockSpec((D, FF_TILE), lambda i, j: (0, j)),
                  pl.BlockSpec((D, FF_TILE), lambda i, j: (0, nf + j)),
                  pl.BlockSpec((FFN_CONV, FF_TILE), lambda i, j: (0, j)),
                  pl.BlockSpec((FFN_CONV, FF_TILE), lambda i, j: (0, nf + j)),
                  pl.BlockSpec((1, FF_TILE), lambda i, j: (0, j)),
                  pl.BlockSpec((1, FF_TILE), lambda i, j: (0, nf + j)),
                  pl.BlockSpec((FF_TILE, D), lambda i, j: (j, 0)),
                  pl.BlockSpec((tm, D), lambda i, j: (i, 0))],
        out_specs=pl.BlockSpec((tm, D), lambda i, j: (i, 0)),
        out_shape=jax.ShapeDtypeStruct((T, D), F32),
        scratch_shapes=[pltpu.VMEM((tm + 8, FF_TILE), F32), pltpu.VMEM((tm + 8, FF_TILE), F32),
                        pltpu.VMEM((tm, FF_TILE), BF16), pltpu.VMEM((tm, D), F32)],
        compiler_params=_cparams("parallel", "arbitrary"),
        name="conv_ffn",
    )(hn, hn, wb, wb, conv_w, conv_w, cb, cb, w_down.astype(BF16), x1)


def kernel(x, mem, pos, mix_norm_g, w_in, ssd_conv_w, ssd_conv_b, ssd_dt_bias, ssd_a_log, ssd_d,
           ssd_norm_g, attn_q_norm_g, attn_k_norm_g, mem_norm_g, w_mem_kv, mem_q_norm_g,
           mem_k_norm_g, w_out, ffn_norm_g, w_up, ffn_conv_w, ffn_conv_b, w_down):
    B, L, D = x.shape
    T = B * L
    depth = w_in.shape[0]
    assert L % max(KEY_CHUNK, 1024 if L >= 1024 else KEY_CHUNK) == 0 and D == 1024
    xf = x.reshape(T, D)
    cos_t, sin_t = _rope_tables(pos)
    for l in range(depth):
        slab = _in_proj(xf, mix_norm_g[l], _slab_weight(w_in[l]))
        y_ssd = _ssd(slab, B, L, ssd_conv_w[l], ssd_conv_b[l], ssd_dt_bias[l], ssd_a_log[l],
                     ssd_d[l], ssd_norm_g[l])
        qn, kn, vt, iqr, ika, ikb, iwt, mqn = _prep(slab, cos_t, sin_t, attn_q_norm_g[l],
                                                    attn_k_norm_g[l], mem_q_norm_g[l])
        mk, mvt = _mem_kv(mem, mem_norm_g[l], w_mem_kv[l], mem_k_norm_g[l])
        y_att = _attention(qn, iqr, iwt, mqn, ika, ikb, kn, vt, mk, mvt, B, L)
        x1, hn = _out_proj(xf, y_ssd, y_att, w_out[l], ffn_norm_g[l])
        xf = _ffn(x1, hn, w_up[l], ffn_conv_w[l], ffn_conv_b[l], w_down[l], L)
    return xf.reshape(B, L, D)
```

```python
import functools

import numpy as np
import jax
import jax.numpy as jnp
from jax import lax
from jax.experimental import pallas as pl
from jax.experimental.pallas import tpu as pltpu

F32 = jnp.float32
BF16 = jnp.bfloat16
I32 = jnp.int32
I16 = jnp.int16

EPS = 1e-6
ROPE_THETA = 500000.0
HEAD_DIM = 64
ROT_HALF = 8
SSD_HEADS = 8
SSD_INNER = 512
SSD_GROUPS = 2
SSD_STATE = 128
SSD_CONV = 4
SSD_CHUNK = 128
SSD_CONV_DIM = 1024
ATTN_INNER = 256
IDX_HEADS = 4
IDX_DIM = 64
TOPK_MAX = 256
Q_BLOCK = 256
N_MEM = 256
MEM_INNER = 256
D_FF = 2816
FFN_CONV = 3

SLAB_W = 2944
SMALL_DT = 64
SMALL_IW = 72
BLK_XBC, BLK_Z, BLK_Q, BLK_K, BLK_V, BLK_IQ, BLK_MQ, BLK_SMALL = 0, 2, 6, 7, 8, 9, 10, 22

KEY_CHUNK = 512
TIE_BLOCK = 256
FF_TILE = 256
FFN_ROWS = 64
FFN_SPLIT = 4
NEG_BIG = -1e30
INT_MIN = -2 ** 31
KEY_NEG_INF = int(np.array(0xFF800000 ^ 0x7FFFFFFF, dtype=np.uint32).astype(np.int32))

VMEM_LIMIT = 56 * 1024 * 1024


def _cparams(*sem):
    return pltpu.CompilerParams(dimension_semantics=sem, vmem_limit_bytes=VMEM_LIMIT)


def _nt(a, b):
    return lax.dot_general(a, b, (((1,), (1,)), ((), ())), preferred_element_type=F32)


def _mm(a, b):
    return jnp.dot(a, b, preferred_element_type=F32)


def _sigmoid(x):
    return 1.0 / (1.0 + jnp.exp(-x))


def _silu(x):
    return x * _sigmoid(x)


def _softplus(x):
    return jnp.maximum(x, 0.0) + jnp.log(1.0 + jnp.exp(-jnp.abs(x)))


def _rope_kernel(pos_ref, invf_ref, sgn_ref, cos_ref, sin_ref):
    ang = pos_ref[...].astype(F32) * invf_ref[...]
    cos_ref[...] = jnp.cos(ang)
    sin_ref[...] = jnp.sin(ang) * sgn_ref[...]


def _rope_tables(pos):
    T = pos.size
    half = ROT_HALF
    inv_freq = ROPE_THETA ** (-(jnp.arange(half, dtype=F32) * 2.0 / (2 * half)))
    lane = np.arange(128) % HEAD_DIM
    rot = lane < 2 * half
    invf = jnp.where(jnp.asarray(rot), inv_freq[jnp.asarray(lane % half)], 0.0).reshape(1, 128)
    sgn = jnp.asarray(np.where(lane < half, -1.0, np.where(rot, 1.0, 0.0)), F32).reshape(1, 128)
    posb = jnp.broadcast_to(pos.reshape(T, 1), (T, 128))
    tm = min(T, 2048)
    return pl.pallas_call(
        _rope_kernel,
        grid=(T // tm,),
        in_specs=[pl.BlockSpec((tm, 128), lambda i: (i, 0)),
                  pl.BlockSpec((1, 128), lambda i: (0, 0)),
                  pl.BlockSpec((1, 128), lambda i: (0, 0))],
        out_specs=[pl.BlockSpec((tm, 128), lambda i: (i, 0)),
                   pl.BlockSpec((tm, 128), lambda i: (i, 0))],
        out_shape=[jax.ShapeDtypeStruct((T, 128), F32)] * 2,
        compiler_params=_cparams("parallel"),
        name="rope_tables",
    )(posb, invf, sgn)


def _rope(x, cos, sin_s):
    w = x.shape[-1]
    if w == 256:
        cos = jnp.concatenate([cos, cos], axis=1)
        sin_s = jnp.concatenate([sin_s, sin_s], axis=1)
    lane = lax.broadcasted_iota(I32, x.shape, 1)
    first = (lane & (HEAD_DIM - 1)) < ROT_HALF
    partner = jnp.where(first, pltpu.roll(x, w - ROT_HALF, 1), pltpu.roll(x, ROT_HALF, 1))
    return x * cos + partner * sin_s


def _seg_ones(n):
    r = lax.broadcasted_iota(I32, (n, n), 0) // HEAD_DIM
    c = lax.broadcasted_iota(I32, (n, n), 1) // HEAD_DIM
    return jnp.where(r == c, 1.0, 0.0).astype(BF16)


def _head_rms(x, seg):
    sq = x * x
    hi = sq.astype(BF16)
    lo = (sq - hi.astype(F32)).astype(BF16)
    ms = (_mm(hi, seg) + _mm(lo, seg)) * (1.0 / HEAD_DIM)
    return x * lax.rsqrt(ms + EPS)


def _in_proj_kernel(x_ref, g_ref, w_ref, o_ref):
    x = x_ref[...]
    r = lax.rsqrt(jnp.mean(x * x, axis=-1, keepdims=True) + EPS)
    h = ((x * r) * g_ref[...]).astype(BF16)
    o_ref[...] = _mm(h, w_ref[...])


def _in_proj(xf, g, w_slab):
    T, D = xf.shape
    tm = min(T, 512)
    return pl.pallas_call(
        _in_proj_kernel,
        grid=(T // tm,),
        in_specs=[pl.BlockSpec((tm, D), lambda i: (i, 0)),
                  pl.BlockSpec((1, D), lambda i: (0, 0)),
                  pl.BlockSpec((D, SLAB_W), lambda i: (0, 0))],
        out_specs=pl.BlockSpec((tm, SLAB_W), lambda i: (i, 0)),
        out_shape=jax.ShapeDtypeStruct((T, SLAB_W), F32),
        compiler_params=_cparams("parallel"),
        name="in_proj",
    )(xf, g.reshape(1, D), w_slab)


def _slab_weight_kernel(w_ref, o_ref):
    rows = 256
    d = w_ref.shape[1]

    def body(r, _):
        rs = pl.ds(pl.multiple_of(r * rows, rows), rows)
        o_ref[rs, 0:1024] = w_ref[0, rs, 512:1536].astype(BF16)
        o_ref[rs, 1024:1536] = w_ref[0, rs, 0:512].astype(BF16)
        o_ref[rs, 1536:2560] = w_ref[0, rs, 1544:2568].astype(BF16)
        o_ref[rs, 2560:2816] = w_ref[0, rs, 2636:2892].astype(BF16)
        small = jnp.concatenate([w_ref[0, rs, 2568:2632], w_ref[0, rs, 1536:1544], w_ref[0, rs, 2632:2636],
                                 jnp.zeros((rows, SLAB_W - 2892), F32)], axis=1)
        o_ref[rs, 2816:SLAB_W] = small.astype(BF16)
        return 0

    lax.fori_loop(0, d // rows, body, 0)


def _slab_weight(w_in, l):
    _, d, n = w_in.shape
    return pl.pallas_call(
        _slab_weight_kernel,
        grid=(1,),
        in_specs=[pl.BlockSpec((1, d, n), lambda i: (l, 0, 0))],
        out_specs=pl.BlockSpec((d, SLAB_W), lambda i: (0, 0)),
        out_shape=jax.ShapeDtypeStruct((d, SLAB_W), BF16),
        compiler_params=_cparams("arbitrary"),
        name="slab_weight",
    )(w_in)


def _ssd_kernel(xbc_ref, z_ref, sm_ref, cw_ref, cb_ref, dtb_ref, alog_ref, dsk_ref, ng_ref,
                o_ref, st_ref, carry_ref, *, ts):
    q = SSD_CHUNK
    p = HEAD_DIM

    @pl.when(pl.program_id(1) == 0)
    def _():
        st_ref[...] = jnp.zeros_like(st_ref)
        carry_ref[...] = jnp.zeros_like(carry_ref)

    cw = cw_ref[...]
    cb = cb_ref[...]
    dtb = dtb_ref[...]
    a_row = -jnp.exp(alog_ref[...])
    dsk = dsk_ref[...]
    ng = ng_ref[...]
    row8 = lax.broadcasted_iota(I32, (8, SSD_CONV_DIM), 0)
    rowi = lax.broadcasted_iota(I32, (q, q), 0)
    coli = lax.broadcasted_iota(I32, (q, q), 1)
    tril = rowi >= coli

    def chunk(c, _):
        off = pl.multiple_of(c * q, q)
        cur = xbc_ref[pl.ds(off, q), :]
        poff = pl.multiple_of(jnp.maximum(off - 8, 0), 8)
        prev8 = jnp.where(c == 0, carry_ref[...], xbc_ref[pl.ds(poff, 8), :])
        acc = cur * cw[SSD_CONV - 1:SSD_CONV, :] + cb
        for k in range(1, SSD_CONV):
            rk = pltpu.roll(cur, k, 0)
            pk = pltpu.roll(prev8, k, 0)
            top = jnp.where(row8 < k, pk, rk[0:8, :])
            sh = jnp.concatenate([top, rk[8:, :]], axis=0)
            acc = acc + sh * cw[SSD_CONV - 1 - k:SSD_CONV - k, :]
        xc = _silu(acc)
        xs = xc[:, :SSD_INNER]

        sm = sm_ref[pl.ds(off, q), :]
        dtv = _softplus(sm + dtb)
        cs = dtv * a_row
        s = 1
        while s < q:
            cs = cs + jnp.where(rowi >= s, pltpu.roll(cs, s, 0), 0.0)
            s *= 2
        cst = cs.T

        ys = []
        for g in range(SSD_GROUPS):
            b0 = SSD_INNER + g * SSD_STATE
            c0 = SSD_INNER + SSD_GROUPS * SSD_STATE + g * SSD_STATE
            bm = xc[:, b0:b0 + SSD_STATE]
            cmb = xc[:, c0:c0 + SSD_STATE].astype(BF16)
            cbm = _nt(cmb, bm.astype(BF16))
            bmt = bm.T.astype(BF16)
            for r in range(SSD_HEADS // SSD_GROUPS):
                h = g * (SSD_HEADS // SSD_GROUPS) + r
                hl = SMALL_DT + h
                colb = jnp.broadcast_to(cs[:, hl:hl + 1], (q, q))
                seg = colb - cst[hl:hl + 1, :]
                lm = jnp.exp(jnp.where(tril, seg, -jnp.inf))
                xs_h = xs[:, h * p:(h + 1) * p]
                xdt = xs_h * jnp.broadcast_to(dtv[:, hl:hl + 1], (q, p))
                yd = _mm((cbm * lm).astype(BF16), xdt.astype(BF16))
                cb64 = colb[:, :p]
                last = jnp.broadcast_to(cb64[q - 1:q, :], (q, p))
                xw = (xdt * jnp.exp(last - cb64)).astype(BF16)
                st_new = _mm(bmt, xw)
                st_prev = st_ref[h]
                yo = _mm(cmb, st_prev.astype(BF16)) * jnp.exp(cb64)
                st_ref[h] = st_prev * jnp.exp(last) + st_new
                ys.append(yd + yo + xs_h * dsk[:, h * p:(h + 1) * p])
        y = jnp.concatenate(ys, axis=1)
        yg = y * _silu(z_ref[pl.ds(off, q), :])
        gw = SSD_INNER // SSD_GROUPS
        outs = []
        for g in range(SSD_GROUPS):
            v = yg[:, g * gw:(g + 1) * gw]
            rr = lax.rsqrt(jnp.mean(v * v, axis=-1, keepdims=True) + EPS)
            outs.append((v * rr) * ng[:, g * gw:(g + 1) * gw])
        o_ref[pl.ds(off, q), :] = jnp.concatenate(outs, axis=1).astype(o_ref.dtype)
        return 0

    lax.fori_loop(0, ts // q, chunk, 0)
    carry_ref[...] = xbc_ref[ts - 8:ts, :]


def _ssd(slab, B, L, conv_w, conv_b, dt_bias, a_log, d_skip, norm_g):
    T = B * L
    ts = min(L, 1024)
    nj = L // ts
    dtb = jnp.zeros((1, 128), F32).at[0, SMALL_DT:SMALL_DT + SSD_HEADS].set(dt_bias)
    alog = jnp.zeros((1, 128), F32).at[0, SMALL_DT:SMALL_DT + SSD_HEADS].set(a_log)
    dsk = jnp.repeat(d_skip, HEAD_DIM).reshape(1, SSD_INNER)
    row = lambda w: pl.BlockSpec((1, w), lambda b, j: (0, 0))
    return pl.pallas_call(
        functools.partial(_ssd_kernel, ts=ts),
        grid=(B, nj),
        in_specs=[pl.BlockSpec((ts, SSD_CONV_DIM), lambda b, j: (b * nj + j, BLK_XBC)),
                  pl.BlockSpec((ts, SSD_INNER), lambda b, j: (b * nj + j, BLK_Z)),
                  pl.BlockSpec((ts, 128), lambda b, j: (b * nj + j, BLK_SMALL)),
                  pl.BlockSpec((SSD_CONV, SSD_CONV_DIM), lambda b, j: (0, 0)),
                  row(SSD_CONV_DIM), row(128), row(128), row(SSD_INNER), row(SSD_INNER)],
        out_specs=pl.BlockSpec((ts, SSD_INNER), lambda b, j: (b * nj + j, 0)),
        out_shape=jax.ShapeDtypeStruct((T, SSD_INNER), BF16),
        scratch_shapes=[pltpu.VMEM((SSD_HEADS, SSD_STATE, HEAD_DIM), F32),
                        pltpu.VMEM((8, SSD_CONV_DIM), F32)],
        compiler_params=_cparams("arbitrary", "arbitrary"),
        name="ssd",
    )(slab, slab, slab, conv_w, conv_b.reshape(1, -1), dtb, alog, dsk, norm_g.reshape(1, -1))


def _prep_kernel(q_ref, k_ref, v_ref, iq_ref, mq_ref, sm_ref, cos_ref, sin_ref,
                 qg_ref, kg_ref, mg_ref,
                 qn_ref, kn_ref, vt_ref, iqr_ref, ika_ref, ikb_ref, iwt_ref, mqn_ref):
    cos = cos_ref[...]
    sin_s = sin_ref[...]
    seg = _seg_ones(ATTN_INNER)
    scale = HEAD_DIM ** -0.5
    qn = _rope(_head_rms(q_ref[...], seg) * qg_ref[...], cos, sin_s) * scale
    qn_ref[...] = qn.astype(BF16)
    kn = _rope(_head_rms(k_ref[...], seg) * kg_ref[...], cos, sin_s)
    kn_ref[...] = kn.astype(BF16)
    vt_ref[0] = v_ref[...].T.astype(BF16)
    iqr_ref[...] = _rope(iq_ref[...], cos, sin_s).astype(BF16)
    mqn_ref[...] = ((_head_rms(mq_ref[...], seg) * mg_ref[...]) * scale).astype(BF16)
    sm = sm_ref[...]
    ikr = _rope(sm, cos, sin_s)
    lane = lax.broadcasted_iota(I32, ikr.shape, 1)
    ika = jnp.where(lane < IDX_DIM, ikr, 0.0)
    ika_ref[...] = ika.astype(BF16)
    ikb_ref[...] = pltpu.roll(ika, IDX_DIM, 1).astype(BF16)
    smt = sm.T
    iwt_ref[...] = smt[SMALL_IW:SMALL_IW + 8, :] * ((IDX_HEADS ** -0.5) * (IDX_DIM ** -0.5))


def _prep(slab, cos_t, sin_t, qg, kg, mg):
    T = slab.shape[0]
    tm = KEY_CHUNK
    tile4 = lambda g: jnp.tile(g, ATTN_INNER // HEAD_DIM).reshape(1, ATTN_INNER)
    col = lambda w, blk: pl.BlockSpec((tm, w), lambda i: (i, blk))
    row = lambda w: pl.BlockSpec((1, w), lambda i: (0, 0))
    out = lambda w: pl.BlockSpec((tm, w), lambda i: (i, 0))
    return pl.pallas_call(
        _prep_kernel,
        grid=(T // tm,),
        in_specs=[col(256, BLK_Q), col(256, BLK_K), col(256, BLK_V), col(256, BLK_IQ),
                  col(256, BLK_MQ), col(128, BLK_SMALL), out(128), out(128),
                  row(256), row(256), row(256)],
        out_specs=[out(256), out(256), pl.BlockSpec((1, 256, tm), lambda i: (i, 0, 0)),
                   out(256), out(128), out(128), pl.BlockSpec((8, tm), lambda i: (0, i)), out(256)],
        out_shape=[jax.ShapeDtypeStruct((T, 256), BF16), jax.ShapeDtypeStruct((T, 256), BF16),
                   jax.ShapeDtypeStruct((T // tm, 256, tm), BF16),
                   jax.ShapeDtypeStruct((T, 256), BF16), jax.ShapeDtypeStruct((T, 128), BF16),
                   jax.ShapeDtypeStruct((T, 128), BF16), jax.ShapeDtypeStruct((8, T), F32),
                   jax.ShapeDtypeStruct((T, 256), BF16)],
        compiler_params=_cparams("parallel"),
        name="attn_prep",
    )(slab, slab, slab, slab, slab, slab, cos_t, sin_t, tile4(qg), tile4(kg), tile4(mg))


def _mem_kv_kernel(mem_ref, g_ref, w_ref, kg_ref, mk_ref, mvt_ref):
    x = mem_ref[...]
    r = lax.rsqrt(jnp.mean(x * x, axis=-1, keepdims=True) + EPS)
    h = ((x * r) * g_ref[...]).astype(BF16)
    kv = _mm(h, w_ref[...])
    seg = _seg_ones(MEM_INNER)
    mk_ref[...] = (_head_rms(kv[:, :MEM_INNER], seg) * kg_ref[...]).astype(BF16)
    mvt_ref[0] = kv[:, MEM_INNER:].T.astype(BF16)


def _mem_kv(mem, g, w, kg):
    B, M, D = mem.shape
    return pl.pallas_call(
        _mem_kv_kernel,
        grid=(B,),
        in_specs=[pl.BlockSpec((M, D), lambda b: (b, 0)),
                  pl.BlockSpec((1, D), lambda b: (0, 0)),
                  pl.BlockSpec((D, 2 * MEM_INNER), lambda b: (0, 0)),
                  pl.BlockSpec((1, MEM_INNER), lambda b: (0, 0))],
        out_specs=[pl.BlockSpec((M, MEM_INNER), lambda b: (b, 0)),
                   pl.BlockSpec((1, MEM_INNER, M), lambda b: (b, 0, 0))],
        out_shape=[jax.ShapeDtypeStruct((B * M, MEM_INNER), BF16),
                   jax.ShapeDtypeStruct((B, MEM_INNER, M), BF16)],
        compiler_params=_cparams("parallel"),
        name="mem_kv",
    )(mem.reshape(B * M, D), g.reshape(1, D), w.astype(BF16),
      jnp.tile(kg, MEM_INNER // HEAD_DIM).reshape(1, MEM_INNER))


def _split_heads(x):
    lane = lax.broadcasted_iota(I32, (Q_BLOCK, 128), 1)
    lo = lane < HEAD_DIM
    zero = jnp.zeros((Q_BLOCK, 128), x.dtype)
    heads = []
    for pr in range(2):
        xp = x[:, pr * 128:(pr + 1) * 128]
        heads += [jnp.where(lo, xp, zero), jnp.where(lo, zero, xp)]
    return heads


def _attn_kernel(qn_ref, iqr_ref, iwt_ref, mqn_ref, ika_ref, ikb_ref, kn_ref, vt_ref, mk_ref, mvt_ref,
                 o_ref, keys_ref, khi_ref, klo_ref, bias_ref, sm_ref, pb_ref, *, topk):
    kc = KEY_CHUNK
    qb = Q_BLOCK
    nh = ATTN_INNER // HEAD_DIM
    i = pl.program_id(1)
    q0 = i * qb
    nch = (q0 + qb + kc - 1) // kc
    iw = iwt_ref[...]
    iq = iqr_ref[...]
    rel = (lax.broadcasted_iota(I32, (kc, qb), 0) - lax.broadcasted_iota(I32, (kc, qb), 1)) - q0

    def chunk_off(c):
        return pl.multiple_of(c * kc, kc)

    def p1(c, _):
        off = chunk_off(c)
        ika = ika_ref[pl.ds(off, kc), :]
        ikb = ikb_ref[pl.ds(off, kc), :]
        isc = iw[0:1, :] * jnp.maximum(_nt(ika, iq[:, :128]), 0.0)
        isc = isc + iw[1:2, :] * jnp.maximum(_nt(ikb, iq[:, :128]), 0.0)
        isc = isc + iw[2:3, :] * jnp.maximum(_nt(ika, iq[:, 128:]), 0.0)
        isc = isc + iw[3:4, :] * jnp.maximum(_nt(ikb, iq[:, 128:]), 0.0)
        isc = jnp.where(rel <= -off, isc, -jnp.inf)
        bits = pltpu.bitcast(isc, I32)
        ks = jnp.where(bits < 0, bits ^ 0x7FFFFFFF, bits)
        keys_ref[pl.ds(off, kc), :] = ks
        khi_ref[pl.ds(off, kc), :] = (ks >> 16).astype(I16)
        klo_ref[pl.ds(off, kc), :] = ((ks & 0xFFFF) - 32768).astype(I16)
        return 0

    lax.fori_loop(0, nch, p1, 0)

    one_b = jnp.ones((kc, qb), BF16)
    zero_b = jnp.zeros((kc, qb), BF16)

    def count16(ref, cand, strict):
        def body(c, acc):
            kk = ref[pl.ds(chunk_off(c), kc), :]
            m = jnp.where(kk > cand if strict else kk >= cand, one_b, zero_b)
            m3 = m.reshape(kc // 16, 16, qb)
            parts = [m3[r] for r in range(kc // 16)]
            while len(parts) > 1:
                parts = [parts[r] + parts[r + 1] for r in range(0, len(parts), 2)]
            return acc + parts[0].astype(F32)
        acc = lax.fori_loop(0, nch, body, jnp.zeros((16, qb), F32))
        return jnp.sum(acc, axis=0, keepdims=True)

    def kth_largest16(ref, need):
        def bis(it, t_u):
            cand_u = t_u | jnp.left_shift(jnp.int32(1), 15 - it)
            cnt = count16(ref, (cand_u - 32768).astype(I16), False)
            return jnp.where(cnt >= need, cand_u, t_u)
        return lax.fori_loop(0, 16, bis, jnp.zeros((1, qb), I32)) - 32768

    t_hi = kth_largest16(khi_ref, float(topk))
    t_hi16 = t_hi.astype(I16)
    gt_hi = count16(khi_ref, t_hi16, True)

    def band(c, _):
        off = chunk_off(c)
        klo_ref[pl.ds(off, kc), :] = jnp.where(khi_ref[pl.ds(off, kc), :] == t_hi16,
                                               klo_ref[pl.ds(off, kc), :], jnp.int16(-32768))
        return 0

    lax.fori_loop(0, nch, band, 0)
    t_lo = kth_largest16(klo_ref, topk - gt_hi)
    gt_lo = count16(klo_ref, t_lo.astype(I16), True)
    t = t_hi * 65536 + (t_lo + 32768)
    r_eq = jnp.where(t == KEY_NEG_INF, 0.0, topk - (gt_hi + gt_lo))

    q_heads = _split_heads(qn_ref[...])
    tb = TIE_BLOCK
    tri = jnp.where(lax.broadcasted_iota(I32, (tb, tb), 0) >= lax.broadcasted_iota(I32, (tb, tb), 1),
                    1.0, 0.0).astype(BF16)

    def p3(c, carry):
        eq_cnt, ms, ls, accs = carry
        off = chunk_off(c)
        for r in range(kc // tb):
            k = keys_ref[pl.ds(off + r * tb, tb), :]
            eq = k == t
            pref = _mm(tri, jnp.where(eq, 1.0, 0.0).astype(BF16)) + eq_cnt
            eq_cnt = pref[tb - 1:tb, :]
            sel = (k > t) | (eq & (pref <= r_eq))
            bias_ref[r * tb:(r + 1) * tb, :] = jnp.where(sel, 0.0, -jnp.inf)
        kn = kn_ref[pl.ds(off, kc), :]
        m_news, alphas = [], []
        for h in range(nh):
            s = _nt(kn[:, (h // 2) * 128:(h // 2 + 1) * 128], q_heads[h]) + bias_ref[...]
            sm_ref[h] = s
            cm = jnp.max(jnp.max(s.reshape(kc // 8, 8, qb), axis=0), axis=0, keepdims=True)
            m_new = jnp.maximum(ms[h], cm)
            m_news.append(m_new)
            alphas.append(jnp.exp(ms[h] - m_new))
        ls2 = []
        for h in range(nh):
            pexp = jnp.exp(sm_ref[h] - m_news[h])
            psum = jnp.sum(jnp.sum(pexp.reshape(kc // 8, 8, qb), axis=0), axis=0, keepdims=True)
            ls2.append(ls[h] * alphas[h] + psum)
            pb_ref[h] = pexp.astype(BF16)
        vt = vt_ref[c]
        accs2 = [accs[h] * alphas[h] + _mm(vt[h * HEAD_DIM:(h + 1) * HEAD_DIM, :], pb_ref[h])
                 for h in range(nh)]
        return eq_cnt, tuple(m_news), tuple(ls2), tuple(accs2)

    init = (jnp.zeros((1, qb), F32),
            tuple(jnp.full((1, qb), NEG_BIG, F32) for _ in range(nh)),
            tuple(jnp.zeros((1, qb), F32) for _ in range(nh)),
            tuple(jnp.zeros((HEAD_DIM, qb), F32) for _ in range(nh)))
    _, _, ls, accs = lax.fori_loop(0, nch, p3, init)
    outs = [accs[h] * (1.0 / ls[h]) for h in range(nh)]

    m_heads = _split_heads(mqn_ref[...])
    mk = mk_ref[...]
    mvt = mvt_ref[0]
    for h in range(MEM_INNER // HEAD_DIM):
        s = _nt(mk[:, (h // 2) * 128:(h // 2 + 1) * 128], m_heads[h])
        mx = jnp.max(s, axis=0, keepdims=True)
        pexp = jnp.exp(s - mx)
        den = jnp.sum(pexp, axis=0, keepdims=True)
        o = _mm(mvt[h * HEAD_DIM:(h + 1) * HEAD_DIM, :], pexp.astype(BF16))
        outs.append(o * (1.0 / den))
    o_ref[...] = jnp.concatenate(outs, axis=0).T.astype(o_ref.dtype)


def _attention(qn, iqr, iwt, mqn, ika, ikb, kn, vt, mk, mvt, B, L):
    T = B * L
    nq = L // Q_BLOCK
    ncl = L // KEY_CHUNK
    topk = min(TOPK_MAX, L // 4)
    assert KEY_CHUNK >= topk and KEY_CHUNK % TIE_BLOCK == 0
    nh = ATTN_INNER // HEAD_DIM
    qblk = lambda w: pl.BlockSpec((Q_BLOCK, w), lambda b, i: (b * nq + i, 0))
    kblk = lambda w: pl.BlockSpec((L, w), lambda b, i: (b, 0))
    return pl.pallas_call(
        functools.partial(_attn_kernel, topk=topk),
        grid=(B, nq),
        in_specs=[qblk(256), qblk(256), pl.BlockSpec((8, Q_BLOCK), lambda b, i: (0, b * nq + i)), qblk(256),
                  kblk(128), kblk(128), kblk(256),
                  pl.BlockSpec((ncl, 256, KEY_CHUNK), lambda b, i: (b, 0, 0)),
                  pl.BlockSpec((N_MEM, MEM_INNER), lambda b, i: (b, 0)),
                  pl.BlockSpec((1, MEM_INNER, N_MEM), lambda b, i: (b, 0, 0))],
        out_specs=pl.BlockSpec((Q_BLOCK, ATTN_INNER + MEM_INNER), lambda b, i: (b * nq + i, 0)),
        out_shape=jax.ShapeDtypeStruct((T, ATTN_INNER + MEM_INNER), BF16),
        scratch_shapes=[pltpu.VMEM((L, Q_BLOCK), I32), pltpu.VMEM((L, Q_BLOCK), I16),
                        pltpu.VMEM((L, Q_BLOCK), I16), pltpu.VMEM((KEY_CHUNK, Q_BLOCK), F32),
                        pltpu.VMEM((nh, KEY_CHUNK, Q_BLOCK), F32),
                        pltpu.VMEM((nh, KEY_CHUNK, Q_BLOCK), BF16)],
        compiler_params=_cparams("parallel", "arbitrary"),
        name="attention",
    )(qn, iqr, iwt, mqn, ika, ikb, kn, vt, mk, mvt)


def _out_proj_kernel(x_ref, ys_ref, ya_ref, w_ref, g_ref, x1_ref, hn_ref):
    w = w_ref[...]
    x1 = x_ref[...] + _mm(ys_ref[...], w[:SSD_INNER, :]) + _mm(ya_ref[...], w[SSD_INNER:, :])
    x1_ref[...] = x1
    r = lax.rsqrt(jnp.mean(x1 * x1, axis=-1, keepdims=True) + EPS)
    hn_ref[...] = ((x1 * r) * g_ref[...]).astype(BF16)


def _out_proj(xf, y_ssd, y_att, w_out, g):
    T, D = xf.shape
    tm = min(T, 512)
    blk = lambda w: pl.BlockSpec((tm, w), lambda i: (i, 0))
    return pl.pallas_call(
        _out_proj_kernel,
        grid=(T // tm,),
        in_specs=[blk(D), blk(SSD_INNER), blk(ATTN_INNER + MEM_INNER),
                  pl.BlockSpec((D, D), lambda i: (0, 0)), pl.BlockSpec((1, D), lambda i: (0, 0))],
        out_specs=[blk(D), blk(D)],
        out_shape=[jax.ShapeDtypeStruct((T, D), F32), jax.ShapeDtypeStruct((T, D), BF16)],
        compiler_params=_cparams("parallel"),
        name="out_proj",
    )(xf, y_ssd, y_att, w_out.astype(BF16), g.reshape(1, D))


def _ffn_kernel(h_ref, halo_ref, wg0_ref, wv0_ref, wgn_ref, wvn_ref, cwg_ref, cwv_ref, cbg_ref, cbv_ref,
                wdp_ref, wdl_ref, x1_ref, o_ref,
                uga_ref, uva_ref, ugb_ref, uvb_ref, aa_ref, ab_ref, acc_ref, *, tiles_per_seq, nf):
    i = pl.program_id(0)
    j = pl.program_id(1)
    halo = halo_ref[...]
    halo = jnp.where(i % tiles_per_seq == 0, jnp.zeros_like(halo), halo)
    tm = h_ref.shape[0]
    rs = FFN_ROWS
    cwg = cwg_ref[...]
    cwv = cwv_ref[...]
    cbg = cbg_ref[...]
    cbv = cbv_ref[...]

    def up(w, u_ref, r0, nrows):
        if r0 == 0:
            u_ref[0:8, :] = _mm(halo, w)
        u_ref[8 + r0:8 + r0 + nrows, :] = _mm(h_ref[r0:r0 + nrows, :], w)

    def conv(u_ref, r0, cw, cb):
        blk = u_ref[r0:r0 + rs + 8, :]
        acc = blk[8:, :] * cw[FFN_CONV - 1:FFN_CONV, :] + cb
        for k in range(1, FFN_CONV):
            acc = acc + blk[8 - k:8 - k + rs, :] * cw[FFN_CONV - 1 - k:FFN_CONV - k, :]
        return acc

    @pl.when(j == 0)
    def _():
        up(wg0_ref[...], uga_ref, 0, tm)
        up(wv0_ref[...], uva_ref, 0, tm)
        ab_ref[...] = jnp.zeros_like(ab_ref)
        acc_ref[...] = jnp.zeros_like(acc_ref)

    def stage(u_cur, u_nxt, a_cur, a_prv):
        qr = tm // FFN_SPLIT
        wgn = wgn_ref[...]
        wvn = wvn_ref[...]
        wdp = wdp_ref[...]
        for q in range(FFN_SPLIT):
            up(wgn, u_nxt[0], q * qr, qr)
            up(wvn, u_nxt[1], q * qr, qr)
            for r0 in range(q * qr, (q + 1) * qr, rs):
                gate = conv(u_cur[0], r0, cwg, cbg)
                val = conv(u_cur[1], r0, cwv, cbv)
                a_cur[r0:r0 + rs, :] = (_silu(gate) * val).astype(BF16)
            acc_ref[q * qr:(q + 1) * qr, :] += _mm(a_prv[q * qr:(q + 1) * qr, :], wdp)

    @pl.when(j % 2 == 0)
    def _():
        stage((uga_ref, uva_ref), (ugb_ref, uvb_ref), aa_ref, ab_ref)

    @pl.when(j % 2 == 1)
    def _():
        stage((ugb_ref, uvb_ref), (uga_ref, uva_ref), ab_ref, aa_ref)

    @pl.when(j == nf - 1)
    def _():
        a_last = aa_ref if (nf - 1) % 2 == 0 else ab_ref
        o_ref[...] = x1_ref[...] + (acc_ref[...] + _mm(a_last[...], wdl_ref[...]))


def _ffn(x1, hn, w_up, conv_w, conv_b, w_down, L):
    T, D = x1.shape
    tm = min(L, 1024)
    nf = D_FF // FF_TILE
    wb = w_up.astype(BF16)
    wd = w_down.astype(BF16)
    cb = conv_b.reshape(1, 2 * D_FF)
    hb = tm // 8
    nxt = lambda j: jnp.minimum(j + 1, nf - 1)
    wspec = lambda f: pl.BlockSpec((D, FF_TILE), f)
    u_scr = pltpu.VMEM((tm + 8, FF_TILE), F32)
    a_scr = pltpu.VMEM((tm, FF_TILE), BF16)
    return pl.pallas_call(
        functools.partial(_ffn_kernel, tiles_per_seq=L // tm, nf=nf),
        grid=(T // tm, nf),
        in_specs=[pl.BlockSpec((tm, D), lambda i, j: (i, 0)),
                  pl.BlockSpec((8, D), lambda i, j: (jnp.maximum(i * hb - 1, 0), 0)),
                  wspec(lambda i, j: (0, 0)), wspec(lambda i, j: (0, nf)),
                  wspec(lambda i, j: (0, nxt(j))), wspec(lambda i, j: (0, nf + nxt(j))),
                  pl.BlockSpec((FFN_CONV, FF_TILE), lambda i, j: (0, j)),
                  pl.BlockSpec((FFN_CONV, FF_TILE), lambda i, j: (0, nf + j)),
                  pl.BlockSpec((1, FF_TILE), lambda i, j: (0, j)),
                  pl.BlockSpec((1, FF_TILE), lambda i, j: (0, nf + j)),
                  pl.BlockSpec((FF_TILE, D), lambda i, j: (jnp.maximum(j - 1, 0), 0)),
                  pl.BlockSpec((FF_TILE, D), lambda i, j: (nf - 1, 0)),
                  pl.BlockSpec((tm, D), lambda i, j: (i, 0))],
        out_specs=pl.BlockSpec((tm, D), lambda i, j: (i, 0)),
        out_shape=jax.ShapeDtypeStruct((T, D), F32),
        scratch_shapes=[u_scr, u_scr, u_scr, u_scr, a_scr, a_scr, pltpu.VMEM((tm, D), F32)],
        compiler_params=_cparams("parallel", "arbitrary"),
        name="conv_ffn",
    )(hn, hn, wb, wb, wb, wb, conv_w, conv_w, cb, cb, wd, wd, x1)


def kernel(x, mem, pos, mix_norm_g, w_in, ssd_conv_w, ssd_conv_b, ssd_dt_bias, ssd_a_log, ssd_d,
           ssd_norm_g, attn_q_norm_g, attn_k_norm_g, mem_norm_g, w_mem_kv, mem_q_norm_g,
           mem_k_norm_g, w_out, ffn_norm_g, w_up, ffn_conv_w, ffn_conv_b, w_down):
    B, L, D = x.shape
    T = B * L
    depth = w_in.shape[0]
    assert L % max(KEY_CHUNK, 1024 if L >= 1024 else KEY_CHUNK) == 0 and D == 1024
    xf = x.reshape(T, D)
    cos_t, sin_t = _rope_tables(pos)
    for l in range(depth):
        slab = _in_proj(xf, mix_norm_g[l], _slab_weight(w_in, l))
        y_ssd = _ssd(slab, B, L, ssd_conv_w[l], ssd_conv_b[l], ssd_dt_bias[l], ssd_a_log[l],
                     ssd_d[l], ssd_norm_g[l])
        qn, kn, vt, iqr, ika, ikb, iwt, mqn = _prep(slab, cos_t, sin_t, attn_q_norm_g[l],
                                                    attn_k_norm_g[l], mem_q_norm_g[l])
        mk, mvt = _mem_kv(mem, mem_norm_g[l], w_mem_kv[l], mem_k_norm_g[l])
        y_att = _attention(qn, iqr, iwt, mqn, ika, ikb, kn, vt, mk, mvt, B, L)
        x1, hn = _out_proj(xf, y_ssd, y_att, w_out[l], ffn_norm_g[l])
        xf = _ffn(x1, hn, w_up[l], ffn_conv_w[l], ffn_conv_b[l], w_down[l], L)
    return xf.reshape(B, L, D)
```

```python
import functools

import numpy as np
import jax
import jax.numpy as jnp
from jax import lax
from jax.experimental import pallas as pl
from jax.experimental.pallas import tpu as pltpu

F32 = jnp.float32
BF16 = jnp.bfloat16
I32 = jnp.int32
I16 = jnp.int16

EPS = 1e-6
ROPE_THETA = 500000.0
HEAD_DIM = 64
ROT_HALF = 8
SSD_HEADS = 8
SSD_INNER = 512
SSD_GROUPS = 2
SSD_STATE = 128
SSD_CONV = 4
SSD_CHUNK = 128
SSD_CONV_DIM = 1024
ATTN_INNER = 256
IDX_HEADS = 4
IDX_DIM = 64
TOPK_MAX = 256
Q_BLOCK = 512
N_MEM = 256
MEM_INNER = 256
D_FF = 2816
FFN_CONV = 3

SLAB_W = 2944
SMALL_DT = 64
SMALL_IW = 72
BLK_XBC, BLK_Z, BLK_Q, BLK_K, BLK_V, BLK_IQ, BLK_MQ, BLK_SMALL = 0, 2, 6, 7, 8, 9, 10, 22

KEY_CHUNK = 512
TIE_BLOCK = 256
FF_TILE = 256
FFN_ROWS = 64
FFN_SPLIT = 1
NEG_BIG = -1e30
LOG2_E = 1.4426950408889634
INT_MIN = -2 ** 31
KEY_NEG_INF = int(np.array(0xFF800000 ^ 0x7FFFFFFF, dtype=np.uint32).astype(np.int32))

VMEM_LIMIT = 56 * 1024 * 1024


def _cparams(*sem):
    return pltpu.CompilerParams(dimension_semantics=sem, vmem_limit_bytes=VMEM_LIMIT)


def _nt(a, b):
    return lax.dot_general(a, b, (((1,), (1,)), ((), ())), preferred_element_type=F32)


def _mm(a, b):
    return jnp.dot(a, b, preferred_element_type=F32)


def _sigmoid(x):
    return 1.0 / (1.0 + jnp.exp(-x))


def _silu(x):
    return x * _sigmoid(x)


def _softplus(x):
    return jnp.maximum(x, 0.0) + jnp.log(1.0 + jnp.exp(-jnp.abs(x)))


def _rope_kernel(pos_ref, invf_ref, sgn_ref, cos_ref, sin_ref):
    ang = pos_ref[...].astype(F32) * invf_ref[...]
    cos_ref[...] = jnp.cos(ang)
    sin_ref[...] = jnp.sin(ang) * sgn_ref[...]


def _rope_tables(pos):
    T = pos.size
    half = ROT_HALF
    inv_freq = ROPE_THETA ** (-(jnp.arange(half, dtype=F32) * 2.0 / (2 * half)))
    lane = np.arange(128) % HEAD_DIM
    rot = lane < 2 * half
    invf = jnp.where(jnp.asarray(rot), inv_freq[jnp.asarray(lane % half)], 0.0).reshape(1, 128)
    sgn = jnp.asarray(np.where(lane < half, -1.0, np.where(rot, 1.0, 0.0)), F32).reshape(1, 128)
    posb = jnp.broadcast_to(pos.reshape(T, 1), (T, 128))
    tm = min(T, 2048)
    return pl.pallas_call(
        _rope_kernel,
        grid=(T // tm,),
        in_specs=[pl.BlockSpec((tm, 128), lambda i: (i, 0)),
                  pl.BlockSpec((1, 128), lambda i: (0, 0)),
                  pl.BlockSpec((1, 128), lambda i: (0, 0))],
        out_specs=[pl.BlockSpec((tm, 128), lambda i: (i, 0)),
                   pl.BlockSpec((tm, 128), lambda i: (i, 0))],
        out_shape=[jax.ShapeDtypeStruct((T, 128), F32)] * 2,
        compiler_params=_cparams("parallel"),
        name="rope_tables",
    )(posb, invf, sgn)


def _rope(x, cos, sin_s):
    w = x.shape[-1]
    if w == 256:
        cos = jnp.concatenate([cos, cos], axis=1)
        sin_s = jnp.concatenate([sin_s, sin_s], axis=1)
    lane = lax.broadcasted_iota(I32, x.shape, 1)
    first = (lane & (HEAD_DIM - 1)) < ROT_HALF
    partner = jnp.where(first, pltpu.roll(x, w - ROT_HALF, 1), pltpu.roll(x, ROT_HALF, 1))
    return x * cos + partner * sin_s


def _seg_ones(n):
    r = lax.broadcasted_iota(I32, (n, n), 0) // HEAD_DIM
    c = lax.broadcasted_iota(I32, (n, n), 1) // HEAD_DIM
    return jnp.where(r == c, 1.0, 0.0).astype(BF16)


def _head_rms(x, seg):
    sq = x * x
    hi = sq.astype(BF16)
    lo = (sq - hi.astype(F32)).astype(BF16)
    ms = (_mm(hi, seg) + _mm(lo, seg)) * (1.0 / HEAD_DIM)
    return x * lax.rsqrt(ms + EPS)


def _in_proj_kernel(x_ref, g_ref, w_ref, o_ref):
    x = x_ref[...]
    r = lax.rsqrt(jnp.mean(x * x, axis=-1, keepdims=True) + EPS)
    h = ((x * r) * g_ref[...]).astype(BF16)
    o_ref[...] = _mm(h, w_ref[...])


def _in_proj(xf, g, w_slab):
    T, D = xf.shape
    tm = min(T, 1024)
    return pl.pallas_call(
        _in_proj_kernel,
        grid=(T // tm,),
        in_specs=[pl.BlockSpec((tm, D), lambda i: (i, 0)),
                  pl.BlockSpec((1, D), lambda i: (0, 0)),
                  pl.BlockSpec((D, SLAB_W), lambda i: (0, 0))],
        out_specs=pl.BlockSpec((tm, SLAB_W), lambda i: (i, 0)),
        out_shape=jax.ShapeDtypeStruct((T, SLAB_W), F32),
        compiler_params=_cparams("parallel"),
        name="in_proj",
    )(xf, g.reshape(1, D), w_slab)


def _slab_weight_kernel(w_ref, o_ref):
    rows = 256
    d = w_ref.shape[1]

    def body(r, _):
        rs = pl.ds(pl.multiple_of(r * rows, rows), rows)
        o_ref[rs, 0:1024] = w_ref[0, rs, 512:1536].astype(BF16)
        o_ref[rs, 1024:1536] = w_ref[0, rs, 0:512].astype(BF16)
        o_ref[rs, 1536:2560] = w_ref[0, rs, 1544:2568].astype(BF16)
        o_ref[rs, 2560:2816] = w_ref[0, rs, 2636:2892].astype(BF16)
        small = jnp.concatenate([w_ref[0, rs, 2568:2632], w_ref[0, rs, 1536:1544], w_ref[0, rs, 2632:2636],
                                 jnp.zeros((rows, SLAB_W - 2892), F32)], axis=1)
        o_ref[rs, 2816:SLAB_W] = small.astype(BF16)
        return 0

    lax.fori_loop(0, d // rows, body, 0)


def _slab_weight(w_in, l):
    _, d, n = w_in.shape
    return pl.pallas_call(
        _slab_weight_kernel,
        grid=(1,),
        in_specs=[pl.BlockSpec((1, d, n), lambda i: (l, 0, 0))],
        out_specs=pl.BlockSpec((d, SLAB_W), lambda i: (0, 0)),
        out_shape=jax.ShapeDtypeStruct((d, SLAB_W), BF16),
        compiler_params=_cparams("arbitrary"),
        name="slab_weight",
    )(w_in)


def _ssd_kernel(xbc_ref, z_ref, sm_ref, cw_ref, cb_ref, dtb_ref, alog_ref, dsk_ref, ng_ref,
                o_ref, st_ref, carry_ref, *, ts):
    q = SSD_CHUNK
    p = HEAD_DIM

    @pl.when(pl.program_id(1) == 0)
    def _():
        st_ref[...] = jnp.zeros_like(st_ref)
        carry_ref[...] = jnp.zeros_like(carry_ref)

    cw = cw_ref[...]
    cb = cb_ref[...]
    dtb = dtb_ref[...]
    a_row = -jnp.exp(alog_ref[...])
    dsk = dsk_ref[...]
    ng = ng_ref[...]
    row8 = lax.broadcasted_iota(I32, (8, SSD_CONV_DIM), 0)
    rowi = lax.broadcasted_iota(I32, (q, q), 0)
    coli = lax.broadcasted_iota(I32, (q, q), 1)
    tril = rowi >= coli

    def chunk(c, _):
        off = pl.multiple_of(c * q, q)
        cur = xbc_ref[pl.ds(off, q), :]
        poff = pl.multiple_of(jnp.maximum(off - 8, 0), 8)
        prev8 = jnp.where(c == 0, carry_ref[...], xbc_ref[pl.ds(poff, 8), :])
        acc = cur * cw[SSD_CONV - 1:SSD_CONV, :] + cb
        for k in range(1, SSD_CONV):
            rk = pltpu.roll(cur, k, 0)
            pk = pltpu.roll(prev8, k, 0)
            top = jnp.where(row8 < k, pk, rk[0:8, :])
            sh = jnp.concatenate([top, rk[8:, :]], axis=0)
            acc = acc + sh * cw[SSD_CONV - 1 - k:SSD_CONV - k, :]
        xc = _silu(acc)
        xs = xc[:, :SSD_INNER]

        sm = sm_ref[pl.ds(off, q), :]
        dtv = _softplus(sm + dtb)
        cs = dtv * a_row
        s = 1
        while s < q:
            cs = cs + jnp.where(rowi >= s, pltpu.roll(cs, s, 0), 0.0)
            s *= 2
        cst = cs.T

        ys = []
        for g in range(SSD_GROUPS):
            b0 = SSD_INNER + g * SSD_STATE
            c0 = SSD_INNER + SSD_GROUPS * SSD_STATE + g * SSD_STATE
            bm = xc[:, b0:b0 + SSD_STATE]
            cmb = xc[:, c0:c0 + SSD_STATE].astype(BF16)
            cbm = _nt(cmb, bm.astype(BF16))
            bmt = bm.T.astype(BF16)
            for r in range(SSD_HEADS // SSD_GROUPS):
                h = g * (SSD_HEADS // SSD_GROUPS) + r
                hl = SMALL_DT + h
                colb = jnp.broadcast_to(cs[:, hl:hl + 1], (q, q))
                seg = colb - cst[hl:hl + 1, :]
                lm = jnp.exp(jnp.where(tril, seg, -jnp.inf))
                xs_h = xs[:, h * p:(h + 1) * p]
                xdt = xs_h * jnp.broadcast_to(dtv[:, hl:hl + 1], (q, p))
                yd = _mm((cbm * lm).astype(BF16), xdt.astype(BF16))
                cb64 = colb[:, :p]
                last = jnp.broadcast_to(cb64[q - 1:q, :], (q, p))
                xw = (xdt * jnp.exp(last - cb64)).astype(BF16)
                st_new = _mm(bmt, xw)
                st_prev = st_ref[h]
                yo = _mm(cmb, st_prev.astype(BF16)) * jnp.exp(cb64)
                st_ref[h] = st_prev * jnp.exp(last) + st_new
                ys.append(yd + yo + xs_h * dsk[:, h * p:(h + 1) * p])
        y = jnp.concatenate(ys, axis=1)
        yg = y * _silu(z_ref[pl.ds(off, q), :])
        gw = SSD_INNER // SSD_GROUPS
        outs = []
        for g in range(SSD_GROUPS):
            v = yg[:, g * gw:(g + 1) * gw]
            rr = lax.rsqrt(jnp.mean(v * v, axis=-1, keepdims=True) + EPS)
            outs.append((v * rr) * ng[:, g * gw:(g + 1) * gw])
        o_ref[pl.ds(off, q), :] = jnp.concatenate(outs, axis=1).astype(o_ref.dtype)
        return 0

    lax.fori_loop(0, ts // q, chunk, 0)
    carry_ref[...] = xbc_ref[ts - 8:ts, :]


def _ssd(slab, B, L, conv_w, conv_b, dt_bias, a_log, d_skip, norm_g):
    T = B * L
    ts = min(L, 1024)
    nj = L // ts
    dtb = jnp.zeros((1, 128), F32).at[0, SMALL_DT:SMALL_DT + SSD_HEADS].set(dt_bias)
    alog = jnp.zeros((1, 128), F32).at[0, SMALL_DT:SMALL_DT + SSD_HEADS].set(a_log)
    dsk = jnp.repeat(d_skip, HEAD_DIM).reshape(1, SSD_INNER)
    row = lambda w: pl.BlockSpec((1, w), lambda b, j: (0, 0))
    return pl.pallas_call(
        functools.partial(_ssd_kernel, ts=ts),
        grid=(B, nj),
        in_specs=[pl.BlockSpec((ts, SSD_CONV_DIM), lambda b, j: (b * nj + j, BLK_XBC)),
                  pl.BlockSpec((ts, SSD_INNER), lambda b, j: (b * nj + j, BLK_Z)),
                  pl.BlockSpec((ts, 128), lambda b, j: (b * nj + j, BLK_SMALL)),
                  pl.BlockSpec((SSD_CONV, SSD_CONV_DIM), lambda b, j: (0, 0)),
                  row(SSD_CONV_DIM), row(128), row(128), row(SSD_INNER), row(SSD_INNER)],
        out_specs=pl.BlockSpec((ts, SSD_INNER), lambda b, j: (b * nj + j, 0)),
        out_shape=jax.ShapeDtypeStruct((T, SSD_INNER), BF16),
        scratch_shapes=[pltpu.VMEM((SSD_HEADS, SSD_STATE, HEAD_DIM), F32),
                        pltpu.VMEM((8, SSD_CONV_DIM), F32)],
        compiler_params=_cparams("arbitrary", "arbitrary"),
        name="ssd",
    )(slab, slab, slab, conv_w, conv_b.reshape(1, -1), dtb, alog, dsk, norm_g.reshape(1, -1))


def _prep_kernel(q_ref, k_ref, v_ref, iq_ref, mq_ref, sm_ref, cos_ref, sin_ref,
                 qg_ref, kg_ref, mg_ref,
                 qn_ref, kn_ref, vt_ref, iqr_ref, ika_ref, ikb_ref, iwt_ref, mqn_ref):
    cos = cos_ref[...]
    sin_s = sin_ref[...]
    seg = _seg_ones(ATTN_INNER)
    scale = (HEAD_DIM ** -0.5) * LOG2_E
    qn = _rope(_head_rms(q_ref[...], seg) * qg_ref[...], cos, sin_s) * scale
    qn_ref[...] = qn.astype(BF16)
    kn = _rope(_head_rms(k_ref[...], seg) * kg_ref[...], cos, sin_s)
    kn_ref[...] = kn.astype(BF16)
    vt_ref[0] = v_ref[...].T.astype(BF16)
    iqr_ref[...] = _rope(iq_ref[...], cos, sin_s).astype(BF16)
    mqn_ref[...] = ((_head_rms(mq_ref[...], seg) * mg_ref[...]) * scale).astype(BF16)
    sm = sm_ref[...]
    ikr = _rope(sm, cos, sin_s)
    lane = lax.broadcasted_iota(I32, ikr.shape, 1)
    ika = jnp.where(lane < IDX_DIM, ikr, 0.0)
    ika_ref[...] = ika.astype(BF16)
    ikb_ref[...] = pltpu.roll(ika, IDX_DIM, 1).astype(BF16)
    smt = sm.T
    iwt_ref[...] = smt[SMALL_IW:SMALL_IW + 8, :] * ((IDX_HEADS ** -0.5) * (IDX_DIM ** -0.5))


def _prep(slab, cos_t, sin_t, qg, kg, mg):
    T = slab.shape[0]
    tm = KEY_CHUNK
    tile4 = lambda g: jnp.tile(g, ATTN_INNER // HEAD_DIM).reshape(1, ATTN_INNER)
    col = lambda w, blk: pl.BlockSpec((tm, w), lambda i: (i, blk))
    row = lambda w: pl.BlockSpec((1, w), lambda i: (0, 0))
    out = lambda w: pl.BlockSpec((tm, w), lambda i: (i, 0))
    return pl.pallas_call(
        _prep_kernel,
        grid=(T // tm,),
        in_specs=[col(256, BLK_Q), col(256, BLK_K), col(256, BLK_V), col(256, BLK_IQ),
                  col(256, BLK_MQ), col(128, BLK_SMALL), out(128), out(128),
                  row(256), row(256), row(256)],
        out_specs=[out(256), out(256), pl.BlockSpec((1, 256, tm), lambda i: (i, 0, 0)),
                   out(256), out(128), out(128), pl.BlockSpec((8, tm), lambda i: (0, i)), out(256)],
        out_shape=[jax.ShapeDtypeStruct((T, 256), BF16), jax.ShapeDtypeStruct((T, 256), BF16),
                   jax.ShapeDtypeStruct((T // tm, 256, tm), BF16),
                   jax.ShapeDtypeStruct((T, 256), BF16), jax.ShapeDtypeStruct((T, 128), BF16),
                   jax.ShapeDtypeStruct((T, 128), BF16), jax.ShapeDtypeStruct((8, T), F32),
                   jax.ShapeDtypeStruct((T, 256), BF16)],
        compiler_params=_cparams("parallel"),
        name="attn_prep",
    )(slab, slab, slab, slab, slab, slab, cos_t, sin_t, tile4(qg), tile4(kg), tile4(mg))


def _mem_kv_kernel(mem_ref, g_ref, w_ref, kg_ref, mk_ref, mvt_ref):
    x = mem_ref[...]
    r = lax.rsqrt(jnp.mean(x * x, axis=-1, keepdims=True) + EPS)
    h = ((x * r) * g_ref[...]).astype(BF16)
    kv = _mm(h, w_ref[...])
    seg = _seg_ones(MEM_INNER)
    mk_ref[...] = (_head_rms(kv[:, :MEM_INNER], seg) * kg_ref[...]).astype(BF16)
    mvt_ref[0] = kv[:, MEM_INNER:].T.astype(BF16)


def _mem_kv(mem, g, w, kg):
    B, M, D = mem.shape
    return pl.pallas_call(
        _mem_kv_kernel,
        grid=(B,),
        in_specs=[pl.BlockSpec((M, D), lambda b: (b, 0)),
                  pl.BlockSpec((1, D), lambda b: (0, 0)),
                  pl.BlockSpec((D, 2 * MEM_INNER), lambda b: (0, 0)),
                  pl.BlockSpec((1, MEM_INNER), lambda b: (0, 0))],
        out_specs=[pl.BlockSpec((M, MEM_INNER), lambda b: (b, 0)),
                   pl.BlockSpec((1, MEM_INNER, M), lambda b: (b, 0, 0))],
        out_shape=[jax.ShapeDtypeStruct((B * M, MEM_INNER), BF16),
                   jax.ShapeDtypeStruct((B, MEM_INNER, M), BF16)],
        compiler_params=_cparams("parallel"),
        name="mem_kv",
    )(mem.reshape(B * M, D), g.reshape(1, D), w.astype(BF16),
      jnp.tile(kg, MEM_INNER // HEAD_DIM).reshape(1, MEM_INNER))


def _split_heads(x):
    lane = lax.broadcasted_iota(I32, (Q_BLOCK, 128), 1)
    lo = lane < HEAD_DIM
    zero = jnp.zeros((Q_BLOCK, 128), x.dtype)
    heads = []
    for pr in range(2):
        xp = x[:, pr * 128:(pr + 1) * 128]
        heads += [jnp.where(lo, xp, zero), jnp.where(lo, zero, xp)]
    return heads


def _attn_kernel(qn_ref, iqr_ref, iwt_ref, mqn_ref, ika_ref, ikb_ref, kn_ref, vt_ref, mk_ref, mvt_ref,
                 o_ref, keys_ref, khi_ref, klo_ref, bias_ref, sm_ref, pb_ref, *, topk):
    kc = KEY_CHUNK
    qb = Q_BLOCK
    nh = ATTN_INNER // HEAD_DIM
    i = pl.program_id(1)
    q0 = i * qb
    nch = (q0 + qb + kc - 1) // kc
    iw = iwt_ref[...]
    iq = iqr_ref[...]
    rel = (lax.broadcasted_iota(I32, (kc, qb), 0) - lax.broadcasted_iota(I32, (kc, qb), 1)) - q0

    def chunk_off(c):
        return pl.multiple_of(c * kc, kc)

    def p1(c, _):
        off = chunk_off(c)
        ika = ika_ref[pl.ds(off, kc), :]
        ikb = ikb_ref[pl.ds(off, kc), :]
        isc = iw[0:1, :] * jnp.maximum(_nt(ika, iq[:, :128]), 0.0)
        isc = isc + iw[1:2, :] * jnp.maximum(_nt(ikb, iq[:, :128]), 0.0)
        isc = isc + iw[2:3, :] * jnp.maximum(_nt(ika, iq[:, 128:]), 0.0)
        isc = isc + iw[3:4, :] * jnp.maximum(_nt(ikb, iq[:, 128:]), 0.0)
        isc = jnp.where(rel <= -off, isc, -jnp.inf)
        bits = pltpu.bitcast(isc, I32)
        ks = jnp.where(bits < 0, bits ^ 0x7FFFFFFF, bits)
        keys_ref[pl.ds(off, kc), :] = ks
        khi_ref[pl.ds(off, kc), :] = (ks >> 16).astype(I16)
        klo_ref[pl.ds(off, kc), :] = (ks ^ 0x8000).astype(I16)
        return 0

    lax.fori_loop(0, nch, p1, 0)

    one_b = jnp.ones((kc, qb), BF16)
    zero_b = jnp.zeros((kc, qb), BF16)

    def count16(ref, cand, strict):
        def body(c, acc):
            kk = ref[pl.ds(chunk_off(c), kc), :]
            m = jnp.where(kk > cand if strict else kk >= cand, one_b, zero_b)
            m3 = m.reshape(kc // 16, 16, qb)
            parts = [m3[r] for r in range(kc // 16)]
            while len(parts) > 1:
                parts = [parts[r] + parts[r + 1] for r in range(0, len(parts), 2)]
            return acc + parts[0].astype(F32)
        acc = lax.fori_loop(0, nch, body, jnp.zeros((16, qb), F32))
        return jnp.sum(acc, axis=0, keepdims=True)

    def kth_largest16(ref, need):
        def bis(it, t_u):
            cand_u = t_u | jnp.left_shift(jnp.int32(1), 15 - it)
            cnt = count16(ref, (cand_u - 32768).astype(I16), False)
            return jnp.where(cnt >= need, cand_u, t_u)
        return lax.fori_loop(0, 16, bis, jnp.zeros((1, qb), I32)) - 32768

    t_hi = kth_largest16(khi_ref, float(topk))
    t_hi16 = t_hi.astype(I16)
    gt_hi = count16(khi_ref, t_hi16, True)

    def band(c, _):
        off = chunk_off(c)
        klo_ref[pl.ds(off, kc), :] = jnp.where(khi_ref[pl.ds(off, kc), :] == t_hi16,
                                               klo_ref[pl.ds(off, kc), :], jnp.int16(-32768))
        return 0

    lax.fori_loop(0, nch, band, 0)
    t_lo = kth_largest16(klo_ref, topk - gt_hi)
    gt_lo = count16(klo_ref, t_lo.astype(I16), True)
    t = t_hi * 65536 + (t_lo + 32768)
    r_eq = jnp.where(t == KEY_NEG_INF, 0.0, topk - (gt_hi + gt_lo))

    q_heads = _split_heads(qn_ref[...])
    tb = TIE_BLOCK
    tri = jnp.where(lax.broadcasted_iota(I32, (tb, tb), 0) >= lax.broadcasted_iota(I32, (tb, tb), 1),
                    1.0, 0.0).astype(BF16)

    def p3(c, carry):
        eq_cnt, ms, ls, accs = carry
        off = chunk_off(c)
        for r in range(kc // tb):
            k = keys_ref[pl.ds(off + r * tb, tb), :]
            eq = k == t
            pref = _mm(tri, jnp.where(eq, 1.0, 0.0).astype(BF16)) + eq_cnt
            eq_cnt = pref[tb - 1:tb, :]
            sel = (k > t) | (eq & (pref <= r_eq))
            bias_ref[r * tb:(r + 1) * tb, :] = jnp.where(sel, 0.0, -jnp.inf)
        kn = kn_ref[pl.ds(off, kc), :]
        m_news, alphas = [], []
        for h in range(nh):
            s = _nt(kn[:, (h // 2) * 128:(h // 2 + 1) * 128], q_heads[h]) + bias_ref[...]
            sm_ref[h] = s
            cm = jnp.max(jnp.max(s.reshape(kc // 8, 8, qb), axis=0), axis=0, keepdims=True)
            m_new = jnp.maximum(ms[h], cm)
            m_news.append(m_new)
            alphas.append(jnp.exp2(ms[h] - m_new))
        ls2 = []
        for h in range(nh):
            pexp = jnp.exp2(sm_ref[h] - m_news[h])
            psum = jnp.sum(jnp.sum(pexp.reshape(kc // 8, 8, qb), axis=0), axis=0, keepdims=True)
            ls2.append(ls[h] * alphas[h] + psum)
            pb_ref[h] = pexp.astype(BF16)
        vt = vt_ref[c]
        accs2 = [accs[h] * alphas[h] + _mm(vt[h * HEAD_DIM:(h + 1) * HEAD_DIM, :], pb_ref[h])
                 for h in range(nh)]
        return eq_cnt, tuple(m_news), tuple(ls2), tuple(accs2)

    init = (jnp.zeros((1, qb), F32),
            tuple(jnp.full((1, qb), NEG_BIG, F32) for _ in range(nh)),
            tuple(jnp.zeros((1, qb), F32) for _ in range(nh)),
            tuple(jnp.zeros((HEAD_DIM, qb), F32) for _ in range(nh)))
    _, _, ls, accs = lax.fori_loop(0, nch, p3, init)
    outs = [accs[h] * (1.0 / ls[h]) for h in range(nh)]

    m_heads = _split_heads(mqn_ref[...])
    mk = mk_ref[...]
    mvt = mvt_ref[0]
    for h in range(MEM_INNER // HEAD_DIM):
        s = _nt(mk[:, (h // 2) * 128:(h // 2 + 1) * 128], m_heads[h])
        mx = jnp.max(s, axis=0, keepdims=True)
        pexp = jnp.exp2(s - mx)
        den = jnp.sum(pexp, axis=0, keepdims=True)
        o = _mm(mvt[h * HEAD_DIM:(h + 1) * HEAD_DIM, :], pexp.astype(BF16))
        outs.append(o * (1.0 / den))
    o_ref[...] = jnp.concatenate(outs, axis=0).T.astype(o_ref.dtype)


def _attention(qn, iqr, iwt, mqn, ika, ikb, kn, vt, mk, mvt, B, L):
    T = B * L
    nq = L // Q_BLOCK
    ncl = L // KEY_CHUNK
    topk = min(TOPK_MAX, L // 4)
    assert KEY_CHUNK >= topk and KEY_CHUNK % TIE_BLOCK == 0
    nh = ATTN_INNER // HEAD_DIM
    qblk = lambda w: pl.BlockSpec((Q_BLOCK, w), lambda b, i: (b * nq + i, 0))
    kblk = lambda w: pl.BlockSpec((L, w), lambda b, i: (b, 0))
    return pl.pallas_call(
        functools.partial(_attn_kernel, topk=topk),
        grid=(B, nq),
        in_specs=[qblk(256), qblk(256), pl.BlockSpec((8, Q_BLOCK), lambda b, i: (0, b * nq + i)), qblk(256),
                  kblk(128), kblk(128), kblk(256),
                  pl.BlockSpec((ncl, 256, KEY_CHUNK), lambda b, i: (b, 0, 0)),
                  pl.BlockSpec((N_MEM, MEM_INNER), lambda b, i: (b, 0)),
                  pl.BlockSpec((1, MEM_INNER, N_MEM), lambda b, i: (b, 0, 0))],
        out_specs=pl.BlockSpec((Q_BLOCK, ATTN_INNER + MEM_INNER), lambda b, i: (b * nq + i, 0)),
        out_shape=jax.ShapeDtypeStruct((T, ATTN_INNER + MEM_INNER), BF16),
        scratch_shapes=[pltpu.VMEM((L, Q_BLOCK), I32), pltpu.VMEM((L, Q_BLOCK), I16),
                        pltpu.VMEM((L, Q_BLOCK), I16), pltpu.VMEM((KEY_CHUNK, Q_BLOCK), F32),
                        pltpu.VMEM((nh, KEY_CHUNK, Q_BLOCK), F32),
                        pltpu.VMEM((nh, KEY_CHUNK, Q_BLOCK), BF16)],
        compiler_params=_cparams("parallel", "arbitrary"),
        name="attention",
    )(qn, iqr, iwt, mqn, ika, ikb, kn, vt, mk, mvt)


def _out_proj_kernel(x_ref, ys_ref, ya_ref, w_ref, g_ref, x1_ref, hn_ref):
    w = w_ref[...]
    x1 = x_ref[...] + _mm(ys_ref[...], w[:SSD_INNER, :]) + _mm(ya_ref[...], w[SSD_INNER:, :])
    x1_ref[...] = x1
    r = lax.rsqrt(jnp.mean(x1 * x1, axis=-1, keepdims=True) + EPS)
    hn_ref[...] = ((x1 * r) * g_ref[...]).astype(BF16)


def _out_proj(xf, y_ssd, y_att, w_out, g):
    T, D = xf.shape
    tm = min(T, 1024)
    blk = lambda w: pl.BlockSpec((tm, w), lambda i: (i, 0))
    return pl.pallas_call(
        _out_proj_kernel,
        grid=(T // tm,),
        in_specs=[blk(D), blk(SSD_INNER), blk(ATTN_INNER + MEM_INNER),
                  pl.BlockSpec((D, D), lambda i: (0, 0)), pl.BlockSpec((1, D), lambda i: (0, 0))],
        out_specs=[blk(D), blk(D)],
        out_shape=[jax.ShapeDtypeStruct((T, D), F32), jax.ShapeDtypeStruct((T, D), BF16)],
        compiler_params=_cparams("parallel"),
        name="out_proj",
    )(xf, y_ssd, y_att, w_out.astype(BF16), g.reshape(1, D))


def _ffn_kernel(h_ref, halo_ref, wg0_ref, wv0_ref, wgn_ref, wvn_ref, cwg_ref, cwv_ref, cbg_ref, cbv_ref,
                wdp_ref, wdl_ref, x1_ref, o_ref,
                uga_ref, uva_ref, ugb_ref, uvb_ref, aa_ref, ab_ref, acc_ref, *, tiles_per_seq, nf):
    i = pl.program_id(0)
    j = pl.program_id(1)
    halo = halo_ref[...]
    halo = jnp.where(i % tiles_per_seq == 0, jnp.zeros_like(halo), halo)
    tm = h_ref.shape[0]
    rs = FFN_ROWS
    cwg = cwg_ref[...]
    cwv = cwv_ref[...]
    cbg = cbg_ref[...]
    cbv = cbv_ref[...]

    def up(w, u_ref, r0, nrows):
        if r0 == 0:
            u_ref[0:8, :] = _mm(halo, w)
        u_ref[8 + r0:8 + r0 + nrows, :] = _mm(h_ref[r0:r0 + nrows, :], w)

    def conv(u_ref, r0, cw, cb):
        blk = u_ref[r0:r0 + rs + 8, :]
        acc = blk[8:, :] * cw[FFN_CONV - 1:FFN_CONV, :] + cb
        for k in range(1, FFN_CONV):
            acc = acc + blk[8 - k:8 - k + rs, :] * cw[FFN_CONV - 1 - k:FFN_CONV - k, :]
        return acc

    @pl.when(j == 0)
    def _():
        up(wg0_ref[...], uga_ref, 0, tm)
        up(wv0_ref[...], uva_ref, 0, tm)
        ab_ref[...] = jnp.zeros_like(ab_ref)
        acc_ref[...] = jnp.zeros_like(acc_ref)

    def stage(u_cur, u_nxt, a_cur, a_prv):
        qr = tm // FFN_SPLIT
        wgn = wgn_ref[...]
        wvn = wvn_ref[...]
        wdp = wdp_ref[...]
        for q in range(FFN_SPLIT):
            up(wgn, u_nxt[0], q * qr, qr)
            up(wvn, u_nxt[1], q * qr, qr)
            for r0 in range(q * qr, (q + 1) * qr, rs):
                gate = conv(u_cur[0], r0, cwg, cbg)
                val = conv(u_cur[1], r0, cwv, cbv)
                a_cur[r0:r0 + rs, :] = (_silu(gate) * val).astype(BF16)
            acc_ref[q * qr:(q + 1) * qr, :] += _mm(a_prv[q * qr:(q + 1) * qr, :], wdp)

    @pl.when(j % 2 == 0)
    def _():
        stage((uga_ref, uva_ref), (ugb_ref, uvb_ref), aa_ref, ab_ref)

    @pl.when(j % 2 == 1)
    def _():
        stage((ugb_ref, uvb_ref), (uga_ref, uva_ref), ab_ref, aa_ref)

    @pl.when(j == nf - 1)
    def _():
        a_last = aa_ref if (nf - 1) % 2 == 0 else ab_ref
        o_ref[...] = x1_ref[...] + (acc_ref[...] + _mm(a_last[...], wdl_ref[...]))


def _ffn(x1, hn, w_up, conv_w, conv_b, w_down, L):
    T, D = x1.shape
    tm = min(L, 1024)
    nf = D_FF // FF_TILE
    wb = w_up.astype(BF16)
    wd = w_down.astype(BF16)
    cb = conv_b.reshape(1, 2 * D_FF)
    hb = tm // 8
    nxt = lambda j: jnp.minimum(j + 1, nf - 1)
    wspec = lambda f: pl.BlockSpec((D, FF_TILE), f)
    u_scr = pltpu.VMEM((tm + 8, FF_TILE), F32)
    a_scr = pltpu.VMEM((tm, FF_TILE), BF16)
    return pl.pallas_call(
        functools.partial(_ffn_kernel, tiles_per_seq=L // tm, nf=nf),
        grid=(T // tm, nf),
        in_specs=[pl.BlockSpec((tm, D), lambda i, j: (i, 0)),
                  pl.BlockSpec((8, D), lambda i, j: (jnp.maximum(i * hb - 1, 0), 0)),
                  wspec(lambda i, j: (0, 0)), wspec(lambda i, j: (0, nf)),
                  wspec(lambda i, j: (0, nxt(j))), wspec(lambda i, j: (0, nf + nxt(j))),
                  pl.BlockSpec((FFN_CONV, FF_TILE), lambda i, j: (0, j)),
                  pl.BlockSpec((FFN_CONV, FF_TILE), lambda i, j: (0, nf + j)),
                  pl.BlockSpec((1, FF_TILE), lambda i, j: (0, j)),
                  pl.BlockSpec((1, FF_TILE), lambda i, j: (0, nf + j)),
                  pl.BlockSpec((FF_TILE, D), lambda i, j: (jnp.maximum(j - 1, 0), 0)),
                  pl.BlockSpec((FF_TILE, D), lambda i, j: (nf - 1, 0)),
                  pl.BlockSpec((tm, D), lambda i, j: (i, 0))],
        out_specs=pl.BlockSpec((tm, D), lambda i, j: (i, 0)),
        out_shape=jax.ShapeDtypeStruct((T, D), F32),
        scratch_shapes=[u_scr, u_scr, u_scr, u_scr, a_scr, a_scr, pltpu.VMEM((tm, D), F32)],
        compiler_params=_cparams("parallel", "arbitrary"),
        name="conv_ffn",
    )(hn, hn, wb, wb, wb, wb, conv_w, conv_w, cb, cb, wd, wd, x1)


def kernel(x, mem, pos, mix_norm_g, w_in, ssd_conv_w, ssd_conv_b, ssd_dt_bias, ssd_a_log, ssd_d,
           ssd_norm_g, attn_q_norm_g, attn_k_norm_g, mem_norm_g, w_mem_kv, mem_q_norm_g,
           mem_k_norm_g, w_out, ffn_norm_g, w_up, ffn_conv_w, ffn_conv_b, w_down):
    B, L, D = x.shape
    T = B * L
    depth = w_in.shape[0]
    assert L % max(KEY_CHUNK, 1024 if L >= 1024 else KEY_CHUNK) == 0 and D == 1024
    xf = x.reshape(T, D)
    cos_t, sin_t = _rope_tables(pos)
    for l in range(depth):
        slab = _in_proj(xf, mix_norm_g[l], _slab_weight(w_in, l))
        y_ssd = _ssd(slab, B, L, ssd_conv_w[l], ssd_conv_b[l], ssd_dt_bias[l], ssd_a_log[l],
                     ssd_d[l], ssd_norm_g[l])
        qn, kn, vt, iqr, ika, ikb, iwt, mqn = _prep(slab, cos_t, sin_t, attn_q_norm_g[l],
                                                    attn_k_norm_g[l], mem_q_norm_g[l])
        mk, mvt = _mem_kv(mem, mem_norm_g[l], w_mem_kv[l], mem_k_norm_g[l])
        y_att = _attention(qn, iqr, iwt, mqn, ika, ikb, kn, vt, mk, mvt, B, L)
        x1, hn = _out_proj(xf, y_ssd, y_att, w_out[l], ffn_norm_g[l])
        xf = _ffn(x1, hn, w_up[l], ffn_conv_w[l], ffn_conv_b[l], w_down[l], L)
    return xf.reshape(B, L, D)
```

```python
import functools

import numpy as np
import jax
import jax.numpy as jnp
from jax import lax
from jax.experimental import pallas as pl
from jax.experimental.pallas import tpu as pltpu

F32 = jnp.float32
BF16 = jnp.bfloat16
I32 = jnp.int32
I16 = jnp.int16

EPS = 1e-6
ROPE_THETA = 500000.0
HEAD_DIM = 64
ROT_HALF = 8
SSD_HEADS = 8
SSD_INNER = 512
SSD_GROUPS = 2
SSD_STATE = 128
SSD_CONV = 4
SSD_CHUNK = 128
SSD_CONV_DIM = 1024
ATTN_INNER = 256
IDX_HEADS = 4
IDX_DIM = 64
TOPK_MAX = 256
Q_BLOCK = 512
N_MEM = 256
MEM_INNER = 256
D_FF = 2816
FFN_CONV = 3

SLAB_W = 2944
SMALL_DT = 64
SMALL_IW = 72
BLK_XBC, BLK_Z, BLK_Q, BLK_K, BLK_V, BLK_IQ, BLK_MQ, BLK_SMALL = 0, 2, 6, 7, 8, 9, 10, 22

KEY_CHUNK = 512
COUNT_LANES = 4
TIE_BLOCK = 256
FF_TILE = 256
FFN_ROWS = 64
FFN_SPLIT = 1
NEG_BIG = -1e30
LOG2_E = 1.4426950408889634
INT_MIN = -2 ** 31
KEY_NEG_INF = int(np.array(0xFF800000 ^ 0x7FFFFFFF, dtype=np.uint32).astype(np.int32))

VMEM_LIMIT = 56 * 1024 * 1024


def _cparams(*sem):
    return pltpu.CompilerParams(dimension_semantics=sem, vmem_limit_bytes=VMEM_LIMIT)


def _nt(a, b):
    return lax.dot_general(a, b, (((1,), (1,)), ((), ())), preferred_element_type=F32)


def _mm(a, b):
    return jnp.dot(a, b, preferred_element_type=F32)


def _sigmoid(x):
    return 1.0 / (1.0 + jnp.exp(-x))


def _silu(x):
    return x * _sigmoid(x)


def _softplus(x):
    return jnp.maximum(x, 0.0) + jnp.log(1.0 + jnp.exp(-jnp.abs(x)))


def _rope_kernel(pos_ref, invf_ref, sgn_ref, cos_ref, sin_ref):
    ang = pos_ref[...].astype(F32) * invf_ref[...]
    cos_ref[...] = jnp.cos(ang)
    sin_ref[...] = jnp.sin(ang) * sgn_ref[...]


def _rope_tables(pos):
    T = pos.size
    half = ROT_HALF
    inv_freq = ROPE_THETA ** (-(jnp.arange(half, dtype=F32) * 2.0 / (2 * half)))
    lane = np.arange(128) % HEAD_DIM
    rot = lane < 2 * half
    invf = jnp.where(jnp.asarray(rot), inv_freq[jnp.asarray(lane % half)], 0.0).reshape(1, 128)
    sgn = jnp.asarray(np.where(lane < half, -1.0, np.where(rot, 1.0, 0.0)), F32).reshape(1, 128)
    posb = jnp.broadcast_to(pos.reshape(T, 1), (T, 128))
    tm = min(T, 2048)
    return pl.pallas_call(
        _rope_kernel,
        grid=(T // tm,),
        in_specs=[pl.BlockSpec((tm, 128), lambda i: (i, 0)),
                  pl.BlockSpec((1, 128), lambda i: (0, 0)),
                  pl.BlockSpec((1, 128), lambda i: (0, 0))],
        out_specs=[pl.BlockSpec((tm, 128), lambda i: (i, 0)),
                   pl.BlockSpec((tm, 128), lambda i: (i, 0))],
        out_shape=[jax.ShapeDtypeStruct((T, 128), F32)] * 2,
        compiler_params=_cparams("parallel"),
        name="rope_tables",
    )(posb, invf, sgn)


def _rope(x, cos, sin_s):
    w = x.shape[-1]
    if w == 256:
        cos = jnp.concatenate([cos, cos], axis=1)
        sin_s = jnp.concatenate([sin_s, sin_s], axis=1)
    lane = lax.broadcasted_iota(I32, x.shape, 1)
    first = (lane & (HEAD_DIM - 1)) < ROT_HALF
    partner = jnp.where(first, pltpu.roll(x, w - ROT_HALF, 1), pltpu.roll(x, ROT_HALF, 1))
    return x * cos + partner * sin_s


def _seg_ones(n):
    r = lax.broadcasted_iota(I32, (n, n), 0) // HEAD_DIM
    c = lax.broadcasted_iota(I32, (n, n), 1) // HEAD_DIM
    return jnp.where(r == c, 1.0, 0.0).astype(BF16)


def _head_rms(x, seg):
    sq = x * x
    hi = sq.astype(BF16)
    lo = (sq - hi.astype(F32)).astype(BF16)
    ms = (_mm(hi, seg) + _mm(lo, seg)) * (1.0 / HEAD_DIM)
    return x * lax.rsqrt(ms + EPS)


def _in_proj_kernel(x_ref, g_ref, w_ref, o_ref):
    x = x_ref[...]
    r = lax.rsqrt(jnp.mean(x * x, axis=-1, keepdims=True) + EPS)
    h = ((x * r) * g_ref[...]).astype(BF16)
    o_ref[...] = _mm(h, w_ref[...])


def _in_proj(xf, g, w_slab):
    T, D = xf.shape
    tm = min(T, 1024)
    return pl.pallas_call(
        _in_proj_kernel,
        grid=(T // tm,),
        in_specs=[pl.BlockSpec((tm, D), lambda i: (i, 0)),
                  pl.BlockSpec((1, D), lambda i: (0, 0)),
                  pl.BlockSpec((D, SLAB_W), lambda i: (0, 0))],
        out_specs=pl.BlockSpec((tm, SLAB_W), lambda i: (i, 0)),
        out_shape=jax.ShapeDtypeStruct((T, SLAB_W), F32),
        compiler_params=_cparams("parallel"),
        name="in_proj",
    )(xf, g.reshape(1, D), w_slab)


def _slab_weight_kernel(w_ref, o_ref):
    rows = 256
    d = w_ref.shape[1]

    def body(r, _):
        rs = pl.ds(pl.multiple_of(r * rows, rows), rows)
        o_ref[rs, 0:1024] = w_ref[0, rs, 512:1536].astype(BF16)
        o_ref[rs, 1024:1536] = w_ref[0, rs, 0:512].astype(BF16)
        o_ref[rs, 1536:2560] = w_ref[0, rs, 1544:2568].astype(BF16)
        o_ref[rs, 2560:2816] = w_ref[0, rs, 2636:2892].astype(BF16)
        small = jnp.concatenate([w_ref[0, rs, 2568:2632], w_ref[0, rs, 1536:1544], w_ref[0, rs, 2632:2636],
                                 jnp.zeros((rows, SLAB_W - 2892), F32)], axis=1)
        o_ref[rs, 2816:SLAB_W] = small.astype(BF16)
        return 0

    lax.fori_loop(0, d // rows, body, 0)


def _slab_weight(w_in, l):
    _, d, n = w_in.shape
    return pl.pallas_call(
        _slab_weight_kernel,
        grid=(1,),
        in_specs=[pl.BlockSpec((1, d, n), lambda i: (l, 0, 0))],
        out_specs=pl.BlockSpec((d, SLAB_W), lambda i: (0, 0)),
        out_shape=jax.ShapeDtypeStruct((d, SLAB_W), BF16),
        compiler_params=_cparams("arbitrary"),
        name="slab_weight",
    )(w_in)


def _ssd_kernel(xbc_ref, z_ref, sm_ref, cw_ref, cb_ref, dtb_ref, alog_ref, dsk_ref, ng_ref,
                o_ref, st_ref, carry_ref, *, ts):
    q = SSD_CHUNK
    p = HEAD_DIM

    @pl.when(pl.program_id(1) == 0)
    def _():
        st_ref[...] = jnp.zeros_like(st_ref)
        carry_ref[...] = jnp.zeros_like(carry_ref)

    cw = cw_ref[...]
    cb = cb_ref[...]
    dtb = dtb_ref[...]
    a_row = -jnp.exp(alog_ref[...])
    dsk = dsk_ref[...]
    ng = ng_ref[...]
    row8 = lax.broadcasted_iota(I32, (8, SSD_CONV_DIM), 0)
    rowi = lax.broadcasted_iota(I32, (q, q), 0)
    coli = lax.broadcasted_iota(I32, (q, q), 1)
    tril = rowi >= coli

    def chunk(c, _):
        off = pl.multiple_of(c * q, q)
        cur = xbc_ref[pl.ds(off, q), :]
        poff = pl.multiple_of(jnp.maximum(off - 8, 0), 8)
        prev8 = jnp.where(c == 0, carry_ref[...], xbc_ref[pl.ds(poff, 8), :])
        acc = cur * cw[SSD_CONV - 1:SSD_CONV, :] + cb
        for k in range(1, SSD_CONV):
            rk = pltpu.roll(cur, k, 0)
            pk = pltpu.roll(prev8, k, 0)
            top = jnp.where(row8 < k, pk, rk[0:8, :])
            sh = jnp.concatenate([top, rk[8:, :]], axis=0)
            acc = acc + sh * cw[SSD_CONV - 1 - k:SSD_CONV - k, :]
        xc = _silu(acc)
        xs = xc[:, :SSD_INNER]

        sm = sm_ref[pl.ds(off, q), :]
        dtv = _softplus(sm + dtb)
        cs = dtv * a_row
        s = 1
        while s < q:
            cs = cs + jnp.where(rowi >= s, pltpu.roll(cs, s, 0), 0.0)
            s *= 2
        cst = cs.T

        ys = []
        for g in range(SSD_GROUPS):
            b0 = SSD_INNER + g * SSD_STATE
            c0 = SSD_INNER + SSD_GROUPS * SSD_STATE + g * SSD_STATE
            bm = xc[:, b0:b0 + SSD_STATE]
            cmb = xc[:, c0:c0 + SSD_STATE].astype(BF16)
            cbm = _nt(cmb, bm.astype(BF16))
            bmt = bm.T.astype(BF16)
            for r in range(SSD_HEADS // SSD_GROUPS):
                h = g * (SSD_HEADS // SSD_GROUPS) + r
                hl = SMALL_DT + h
                colb = jnp.broadcast_to(cs[:, hl:hl + 1], (q, q))
                seg = colb - cst[hl:hl + 1, :]
                lm = jnp.exp(jnp.where(tril, seg, -jnp.inf))
                xs_h = xs[:, h * p:(h + 1) * p]
                xdt = xs_h * jnp.broadcast_to(dtv[:, hl:hl + 1], (q, p))
                yd = _mm((cbm * lm).astype(BF16), xdt.astype(BF16))
                cb64 = colb[:, :p]
                last = jnp.broadcast_to(cb64[q - 1:q, :], (q, p))
                xw = (xdt * jnp.exp(last - cb64)).astype(BF16)
                st_new = _mm(bmt, xw)
                st_prev = st_ref[h]
                yo = _mm(cmb, st_prev.astype(BF16)) * jnp.exp(cb64)
                st_ref[h] = st_prev * jnp.exp(last) + st_new
                ys.append(yd + yo + xs_h * dsk[:, h * p:(h + 1) * p])
        y = jnp.concatenate(ys, axis=1)
        yg = y * _silu(z_ref[pl.ds(off, q), :])
        gw = SSD_INNER // SSD_GROUPS
        outs = []
        for g in range(SSD_GROUPS):
            v = yg[:, g * gw:(g + 1) * gw]
            rr = lax.rsqrt(jnp.mean(v * v, axis=-1, keepdims=True) + EPS)
            outs.append((v * rr) * ng[:, g * gw:(g + 1) * gw])
        o_ref[pl.ds(off, q), :] = jnp.concatenate(outs, axis=1).astype(o_ref.dtype)
        return 0

    lax.fori_loop(0, ts // q, chunk, 0)
    carry_ref[...] = xbc_ref[ts - 8:ts, :]


def _ssd(slab, B, L, conv_w, conv_b, dt_bias, a_log, d_skip, norm_g):
    T = B * L
    ts = min(L, 1024)
    nj = L // ts
    dtb = jnp.zeros((1, 128), F32).at[0, SMALL_DT:SMALL_DT + SSD_HEADS].set(dt_bias)
    alog = jnp.zeros((1, 128), F32).at[0, SMALL_DT:SMALL_DT + SSD_HEADS].set(a_log)
    dsk = jnp.repeat(d_skip, HEAD_DIM).reshape(1, SSD_INNER)
    row = lambda w: pl.BlockSpec((1, w), lambda b, j: (0, 0))
    return pl.pallas_call(
        functools.partial(_ssd_kernel, ts=ts),
        grid=(B, nj),
        in_specs=[pl.BlockSpec((ts, SSD_CONV_DIM), lambda b, j: (b * nj + j, BLK_XBC)),
                  pl.BlockSpec((ts, SSD_INNER), lambda b, j: (b * nj + j, BLK_Z)),
                  pl.BlockSpec((ts, 128), lambda b, j: (b * nj + j, BLK_SMALL)),
                  pl.BlockSpec((SSD_CONV, SSD_CONV_DIM), lambda b, j: (0, 0)),
                  row(SSD_CONV_DIM), row(128), row(128), row(SSD_INNER), row(SSD_INNER)],
        out_specs=pl.BlockSpec((ts, SSD_INNER), lambda b, j: (b * nj + j, 0)),
        out_shape=jax.ShapeDtypeStruct((T, SSD_INNER), BF16),
        scratch_shapes=[pltpu.VMEM((SSD_HEADS, SSD_STATE, HEAD_DIM), F32),
                        pltpu.VMEM((8, SSD_CONV_DIM), F32)],
        compiler_params=_cparams("arbitrary", "arbitrary"),
        name="ssd",
    )(slab, slab, slab, conv_w, conv_b.reshape(1, -1), dtb, alog, dsk, norm_g.reshape(1, -1))


def _prep_kernel(q_ref, k_ref, v_ref, iq_ref, mq_ref, sm_ref, cos_ref, sin_ref,
                 qg_ref, kg_ref, mg_ref,
                 qn_ref, kn_ref, vt_ref, iqr_ref, ika_ref, ikb_ref, iwt_ref, mqn_ref):
    cos = cos_ref[...]
    sin_s = sin_ref[...]
    seg = _seg_ones(ATTN_INNER)
    scale = (HEAD_DIM ** -0.5) * LOG2_E
    qn = _rope(_head_rms(q_ref[...], seg) * qg_ref[...], cos, sin_s) * scale
    qn_ref[...] = qn.astype(BF16)
    kn = _rope(_head_rms(k_ref[...], seg) * kg_ref[...], cos, sin_s)
    kn_ref[...] = kn.astype(BF16)
    vt_ref[0] = v_ref[...].T.astype(BF16)
    iqr_ref[...] = _rope(iq_ref[...], cos, sin_s).astype(BF16)
    mqn_ref[...] = ((_head_rms(mq_ref[...], seg) * mg_ref[...]) * scale).astype(BF16)
    sm = sm_ref[...]
    ikr = _rope(sm, cos, sin_s)
    lane = lax.broadcasted_iota(I32, ikr.shape, 1)
    ika = jnp.where(lane < IDX_DIM, ikr, 0.0)
    ika_ref[...] = ika.astype(BF16)
    ikb_ref[...] = pltpu.roll(ika, IDX_DIM, 1).astype(BF16)
    smt = sm.T
    iwt_ref[...] = smt[SMALL_IW:SMALL_IW + 8, :] * ((IDX_HEADS ** -0.5) * (IDX_DIM ** -0.5))


def _prep(slab, cos_t, sin_t, qg, kg, mg):
    T = slab.shape[0]
    tm = KEY_CHUNK
    tile4 = lambda g: jnp.tile(g, ATTN_INNER // HEAD_DIM).reshape(1, ATTN_INNER)
    col = lambda w, blk: pl.BlockSpec((tm, w), lambda i: (i, blk))
    row = lambda w: pl.BlockSpec((1, w), lambda i: (0, 0))
    out = lambda w: pl.BlockSpec((tm, w), lambda i: (i, 0))
    return pl.pallas_call(
        _prep_kernel,
        grid=(T // tm,),
        in_specs=[col(256, BLK_Q), col(256, BLK_K), col(256, BLK_V), col(256, BLK_IQ),
                  col(256, BLK_MQ), col(128, BLK_SMALL), out(128), out(128),
                  row(256), row(256), row(256)],
        out_specs=[out(256), out(256), pl.BlockSpec((1, 256, tm), lambda i: (i, 0, 0)),
                   out(256), out(128), out(128), pl.BlockSpec((8, tm), lambda i: (0, i)), out(256)],
        out_shape=[jax.ShapeDtypeStruct((T, 256), BF16), jax.ShapeDtypeStruct((T, 256), BF16),
                   jax.ShapeDtypeStruct((T // tm, 256, tm), BF16),
                   jax.ShapeDtypeStruct((T, 256), BF16), jax.ShapeDtypeStruct((T, 128), BF16),
                   jax.ShapeDtypeStruct((T, 128), BF16), jax.ShapeDtypeStruct((8, T), F32),
                   jax.ShapeDtypeStruct((T, 256), BF16)],
        compiler_params=_cparams("parallel"),
        name="attn_prep",
    )(slab, slab, slab, slab, slab, slab, cos_t, sin_t, tile4(qg), tile4(kg), tile4(mg))


def _mem_kv_kernel(mem_ref, g_ref, w_ref, kg_ref, mk_ref, mvt_ref):
    x = mem_ref[...]
    r = lax.rsqrt(jnp.mean(x * x, axis=-1, keepdims=True) + EPS)
    h = ((x * r) * g_ref[...]).astype(BF16)
    kv = _mm(h, w_ref[...])
    seg = _seg_ones(MEM_INNER)
    mk_ref[...] = (_head_rms(kv[:, :MEM_INNER], seg) * kg_ref[...]).astype(BF16)
    mvt_ref[0] = kv[:, MEM_INNER:].T.astype(BF16)


def _mem_kv(mem, g, w, kg):
    B, M, D = mem.shape
    return pl.pallas_call(
        _mem_kv_kernel,
        grid=(B,),
        in_specs=[pl.BlockSpec((M, D), lambda b: (b, 0)),
                  pl.BlockSpec((1, D), lambda b: (0, 0)),
                  pl.BlockSpec((D, 2 * MEM_INNER), lambda b: (0, 0)),
                  pl.BlockSpec((1, MEM_INNER), lambda b: (0, 0))],
        out_specs=[pl.BlockSpec((M, MEM_INNER), lambda b: (b, 0)),
                   pl.BlockSpec((1, MEM_INNER, M), lambda b: (b, 0, 0))],
        out_shape=[jax.ShapeDtypeStruct((B * M, MEM_INNER), BF16),
                   jax.ShapeDtypeStruct((B, MEM_INNER, M), BF16)],
        compiler_params=_cparams("parallel"),
        name="mem_kv",
    )(mem.reshape(B * M, D), g.reshape(1, D), w.astype(BF16),
      jnp.tile(kg, MEM_INNER // HEAD_DIM).reshape(1, MEM_INNER))


def _split_heads(x):
    lane = lax.broadcasted_iota(I32, (Q_BLOCK, 128), 1)
    lo = lane < HEAD_DIM
    zero = jnp.zeros((Q_BLOCK, 128), x.dtype)
    heads = []
    for pr in range(2):
        xp = x[:, pr * 128:(pr + 1) * 128]
        heads += [jnp.where(lo, xp, zero), jnp.where(lo, zero, xp)]
    return heads


def _attn_kernel(qn_ref, iqr_ref, iwt_ref, mqn_ref, ika_ref, ikb_ref, kn_ref, vt_ref, mk_ref, mvt_ref,
                 o_ref, keys_ref, khi_ref, klo_ref, bias_ref, sm_ref, pb_ref, *, topk):
    kc = KEY_CHUNK
    qb = Q_BLOCK
    nh = ATTN_INNER // HEAD_DIM
    i = pl.program_id(1)
    q0 = i * qb
    nch = (q0 + qb + kc - 1) // kc
    iw = iwt_ref[...]
    iq = iqr_ref[...]
    rel = (lax.broadcasted_iota(I32, (kc, qb), 0) - lax.broadcasted_iota(I32, (kc, qb), 1)) - q0

    def chunk_off(c):
        return pl.multiple_of(c * kc, kc)

    def p1(c, _):
        off = chunk_off(c)
        ika = ika_ref[pl.ds(off, kc), :]
        ikb = ikb_ref[pl.ds(off, kc), :]
        isc = iw[0:1, :] * jnp.maximum(_nt(ika, iq[:, :128]), 0.0)
        isc = isc + iw[1:2, :] * jnp.maximum(_nt(ikb, iq[:, :128]), 0.0)
        isc = isc + iw[2:3, :] * jnp.maximum(_nt(ika, iq[:, 128:]), 0.0)
        isc = isc + iw[3:4, :] * jnp.maximum(_nt(ikb, iq[:, 128:]), 0.0)
        isc = jnp.where(rel <= -off, isc, -jnp.inf)
        bits = pltpu.bitcast(isc, I32)
        ks = jnp.where(bits < 0, bits ^ 0x7FFFFFFF, bits)
        keys_ref[pl.ds(off, kc), :] = ks
        khi_ref[pl.ds(off, kc), :] = (ks >> 16).astype(I16)
        klo_ref[pl.ds(off, kc), :] = (ks ^ 0x8000).astype(I16)
        return 0

    lax.fori_loop(0, nch, p1, 0)

    one_b = jnp.ones((16, qb), BF16)
    zero_b = jnp.zeros((16, qb), BF16)

    def count16(ref, cand, strict):
        def body(c, acc):
            off = chunk_off(c)
            parts = [zero_b] * COUNT_LANES
            for r in range(kc // 16):
                kk = ref[pl.ds(off + r * 16, 16), :]
                m = jnp.where(kk > cand if strict else kk >= cand, one_b, zero_b)
                parts[r % COUNT_LANES] = parts[r % COUNT_LANES] + m
            while len(parts) > 1:
                parts = [parts[r] + parts[r + 1] for r in range(0, len(parts), 2)]
            return acc + parts[0].astype(F32)
        acc = lax.fori_loop(0, nch, body, jnp.zeros((16, qb), F32))
        return jnp.sum(acc, axis=0, keepdims=True)

    def kth_largest16(ref, need):
        def bis(it, t_u):
            cand_u = t_u | jnp.left_shift(jnp.int32(1), 15 - it)
            cnt = count16(ref, (cand_u - 32768).astype(I16), False)
            return jnp.where(cnt >= need, cand_u, t_u)
        return lax.fori_loop(0, 16, bis, jnp.zeros((1, qb), I32)) - 32768

    t_hi = kth_largest16(khi_ref, float(topk))
    t_hi16 = t_hi.astype(I16)
    gt_hi = count16(khi_ref, t_hi16, True)

    def band(c, _):
        off = chunk_off(c)
        klo_ref[pl.ds(off, kc), :] = jnp.where(khi_ref[pl.ds(off, kc), :] == t_hi16,
                                               klo_ref[pl.ds(off, kc), :], jnp.int16(-32768))
        return 0

    lax.fori_loop(0, nch, band, 0)
    t_lo = kth_largest16(klo_ref, topk - gt_hi)
    gt_lo = count16(klo_ref, t_lo.astype(I16), True)
    t = t_hi * 65536 + (t_lo + 32768)
    r_eq = jnp.where(t == KEY_NEG_INF, 0.0, topk - (gt_hi + gt_lo))

    q_heads = _split_heads(qn_ref[...])
    tb = TIE_BLOCK
    tri = jnp.where(lax.broadcasted_iota(I32, (tb, tb), 0) >= lax.broadcasted_iota(I32, (tb, tb), 1),
                    1.0, 0.0).astype(BF16)

    def p3(c, carry):
        eq_cnt, ms, ls, accs = carry
        off = chunk_off(c)
        for r in range(kc // tb):
            k = keys_ref[pl.ds(off + r * tb, tb), :]
            eq = k == t
            pref = _mm(tri, jnp.where(eq, 1.0, 0.0).astype(BF16)) + eq_cnt
            eq_cnt = pref[tb - 1:tb, :]
            sel = (k > t) | (eq & (pref <= r_eq))
            bias_ref[r * tb:(r + 1) * tb, :] = jnp.where(sel, 0.0, -jnp.inf)
        kn = kn_ref[pl.ds(off, kc), :]
        m_news, alphas = [], []
        for h in range(nh):
            s = _nt(kn[:, (h // 2) * 128:(h // 2 + 1) * 128], q_heads[h]) + bias_ref[...]
            sm_ref[h] = s
            cm = jnp.max(jnp.max(s.reshape(kc // 8, 8, qb), axis=0), axis=0, keepdims=True)
            m_new = jnp.maximum(ms[h], cm)
            m_news.append(m_new)
            alphas.append(jnp.exp2(ms[h] - m_new))
        ls2 = []
        for h in range(nh):
            pexp = jnp.exp2(sm_ref[h] - m_news[h])
            psum = jnp.sum(jnp.sum(pexp.reshape(kc // 8, 8, qb), axis=0), axis=0, keepdims=True)
            ls2.append(ls[h] * alphas[h] + psum)
            pb_ref[h] = pexp.astype(BF16)
        vt = vt_ref[c]
        accs2 = [accs[h] * alphas[h] + _mm(vt[h * HEAD_DIM:(h + 1) * HEAD_DIM, :], pb_ref[h])
                 for h in range(nh)]
        return eq_cnt, tuple(m_news), tuple(ls2), tuple(accs2)

    init = (jnp.zeros((1, qb), F32),
            tuple(jnp.full((1, qb), NEG_BIG, F32) for _ in range(nh)),
            tuple(jnp.zeros((1, qb), F32) for _ in range(nh)),
            tuple(jnp.zeros((HEAD_DIM, qb), F32) for _ in range(nh)))
    _, _, ls, accs = lax.fori_loop(0, nch, p3, init)
    outs = [accs[h] * (1.0 / ls[h]) for h in range(nh)]

    m_heads = _split_heads(mqn_ref[...])
    mk = mk_ref[...]
    mvt = mvt_ref[0]
    for h in range(MEM_INNER // HEAD_DIM):
        s = _nt(mk[:, (h // 2) * 128:(h // 2 + 1) * 128], m_heads[h])
        mx = jnp.max(s, axis=0, keepdims=True)
        pexp = jnp.exp2(s - mx)
        den = jnp.sum(pexp, axis=0, keepdims=True)
        o = _mm(mvt[h * HEAD_DIM:(h + 1) * HEAD_DIM, :], pexp.astype(BF16))
        outs.append(o * (1.0 / den))
    o_ref[...] = jnp.concatenate(outs, axis=0).T.astype(o_ref.dtype)


def _attention(qn, iqr, iwt, mqn, ika, ikb, kn, vt, mk, mvt, B, L):
    T = B * L
    nq = L // Q_BLOCK
    ncl = L // KEY_CHUNK
    topk = min(TOPK_MAX, L // 4)
    assert KEY_CHUNK >= topk and KEY_CHUNK % TIE_BLOCK == 0
    nh = ATTN_INNER // HEAD_DIM
    qblk = lambda w: pl.BlockSpec((Q_BLOCK, w), lambda b, i: (b * nq + i, 0))
    kblk = lambda w: pl.BlockSpec((L, w), lambda b, i: (b, 0))
    return pl.pallas_call(
        functools.partial(_attn_kernel, topk=topk),
        grid=(B, nq),
        in_specs=[qblk(256), qblk(256), pl.BlockSpec((8, Q_BLOCK), lambda b, i: (0, b * nq + i)), qblk(256),
                  kblk(128), kblk(128), kblk(256),
                  pl.BlockSpec((ncl, 256, KEY_CHUNK), lambda b, i: (b, 0, 0)),
                  pl.BlockSpec((N_MEM, MEM_INNER), lambda b, i: (b, 0)),
                  pl.BlockSpec((1, MEM_INNER, N_MEM), lambda b, i: (b, 0, 0))],
        out_specs=pl.BlockSpec((Q_BLOCK, ATTN_INNER + MEM_INNER), lambda b, i: (b * nq + i, 0)),
        out_shape=jax.ShapeDtypeStruct((T, ATTN_INNER + MEM_INNER), BF16),
        scratch_shapes=[pltpu.VMEM((L, Q_BLOCK), I32), pltpu.VMEM((L, Q_BLOCK), I16),
                        pltpu.VMEM((L, Q_BLOCK), I16), pltpu.VMEM((KEY_CHUNK, Q_BLOCK), F32),
                        pltpu.VMEM((nh, KEY_CHUNK, Q_BLOCK), F32),
                        pltpu.VMEM((nh, KEY_CHUNK, Q_BLOCK), BF16)],
        compiler_params=_cparams("parallel", "arbitrary"),
        name="attention",
    )(qn, iqr, iwt, mqn, ika, ikb, kn, vt, mk, mvt)


def _out_proj_kernel(x_ref, ys_ref, ya_ref, w_ref, g_ref, x1_ref, hn_ref):
    w = w_ref[...]
    x1 = x_ref[...] + _mm(ys_ref[...], w[:SSD_INNER, :]) + _mm(ya_ref[...], w[SSD_INNER:, :])
    x1_ref[...] = x1
    r = lax.rsqrt(jnp.mean(x1 * x1, axis=-1, keepdims=True) + EPS)
    hn_ref[...] = ((x1 * r) * g_ref[...]).astype(BF16)


def _out_proj(xf, y_ssd, y_att, w_out, g):
    T, D = xf.shape
    tm = min(T, 1024)
    blk = lambda w: pl.BlockSpec((tm, w), lambda i: (i, 0))
    return pl.pallas_call(
        _out_proj_kernel,
        grid=(T // tm,),
        in_specs=[blk(D), blk(SSD_INNER), blk(ATTN_INNER + MEM_INNER),
                  pl.BlockSpec((D, D), lambda i: (0, 0)), pl.BlockSpec((1, D), lambda i: (0, 0))],
        out_specs=[blk(D), blk(D)],
        out_shape=[jax.ShapeDtypeStruct((T, D), F32), jax.ShapeDtypeStruct((T, D), BF16)],
        compiler_params=_cparams("parallel"),
        name="out_proj",
    )(xf, y_ssd, y_att, w_out.astype(BF16), g.reshape(1, D))


def _ffn_kernel(h_ref, halo_ref, wg0_ref, wv0_ref, wgn_ref, wvn_ref, cwg_ref, cwv_ref, cbg_ref, cbv_ref,
                wdp_ref, wdl_ref, x1_ref, o_ref,
                uga_ref, uva_ref, ugb_ref, uvb_ref, aa_ref, ab_ref, acc_ref, *, tiles_per_seq, nf):
    i = pl.program_id(0)
    j = pl.program_id(1)
    halo = halo_ref[...]
    halo = jnp.where(i % tiles_per_seq == 0, jnp.zeros_like(halo), halo)
    tm = h_ref.shape[0]
    rs = FFN_ROWS
    cwg = cwg_ref[...]
    cwv = cwv_ref[...]
    cbg = cbg_ref[...]
    cbv = cbv_ref[...]

    def up(w, u_ref, r0, nrows):
        if r0 == 0:
            u_ref[0:8, :] = _mm(halo, w)
        u_ref[8 + r0:8 + r0 + nrows, :] = _mm(h_ref[r0:r0 + nrows, :], w)

    def conv(u_ref, r0, cw, cb):
        blk = u_ref[r0:r0 + rs + 8, :]
        acc = blk[8:, :] * cw[FFN_CONV - 1:FFN_CONV, :] + cb
        for k in range(1, FFN_CONV):
            acc = acc + blk[8 - k:8 - k + rs, :] * cw[FFN_CONV - 1 - k:FFN_CONV - k, :]
        return acc

    @pl.when(j == 0)
    def _():
        up(wg0_ref[...], uga_ref, 0, tm)
        up(wv0_ref[...], uva_ref, 0, tm)
        ab_ref[...] = jnp.zeros_like(ab_ref)
        acc_ref[...] = jnp.zeros_like(acc_ref)

    def stage(u_cur, u_nxt, a_cur, a_prv):
        qr = tm // FFN_SPLIT
        wgn = wgn_ref[...]
        wvn = wvn_ref[...]
        wdp = wdp_ref[...]
        for q in range(FFN_SPLIT):
            up(wgn, u_nxt[0], q * qr, qr)
            up(wvn, u_nxt[1], q * qr, qr)
            for r0 in range(q * qr, (q + 1) * qr, rs):
                gate = conv(u_cur[0], r0, cwg, cbg)
                val = conv(u_cur[1], r0, cwv, cbv)
                a_cur[r0:r0 + rs, :] = (_silu(gate) * val).astype(BF16)
            acc_ref[q * qr:(q + 1) * qr, :] += _mm(a_prv[q * qr:(q + 1) * qr, :], wdp)

    @pl.when(j % 2 == 0)
    def _():
        stage((uga_ref, uva_ref), (ugb_ref, uvb_ref), aa_ref, ab_ref)

    @pl.when(j % 2 == 1)
    def _():
        stage((ugb_ref, uvb_ref), (uga_ref, uva_ref), ab_ref, aa_ref)

    @pl.when(j == nf - 1)
    def _():
        a_last = aa_ref if (nf - 1) % 2 == 0 else ab_ref
        o_ref[...] = x1_ref[...] + (acc_ref[...] + _mm(a_last[...], wdl_ref[...]))


def _ffn(x1, hn, w_up, conv_w, conv_b, w_down, L):
    T, D = x1.shape
    tm = min(L, 1024)
    nf = D_FF // FF_TILE
    wb = w_up.astype(BF16)
    wd = w_down.astype(BF16)
    cb = conv_b.reshape(1, 2 * D_FF)
    hb = tm // 8
    nxt = lambda j: jnp.minimum(j + 1, nf - 1)
    wspec = lambda f: pl.BlockSpec((D, FF_TILE), f)
    u_scr = pltpu.VMEM((tm + 8, FF_TILE), F32)
    a_scr = pltpu.VMEM((tm, FF_TILE), BF16)
    return pl.pallas_call(
        functools.partial(_ffn_kernel, tiles_per_seq=L // tm, nf=nf),
        grid=(T // tm, nf),
        in_specs=[pl.BlockSpec((tm, D), lambda i, j: (i, 0)),
                  pl.BlockSpec((8, D), lambda i, j: (jnp.maximum(i * hb - 1, 0), 0)),
                  wspec(lambda i, j: (0, 0)), wspec(lambda i, j: (0, nf)),
                  wspec(lambda i, j: (0, nxt(j))), wspec(lambda i, j: (0, nf + nxt(j))),
                  pl.BlockSpec((FFN_CONV, FF_TILE), lambda i, j: (0, j)),
                  pl.BlockSpec((FFN_CONV, FF_TILE), lambda i, j: (0, nf + j)),
                  pl.BlockSpec((1, FF_TILE), lambda i, j: (0, j)),
                  pl.BlockSpec((1, FF_TILE), lambda i, j: (0, nf + j)),
                  pl.BlockSpec((FF_TILE, D), lambda i, j: (jnp.maximum(j - 1, 0), 0)),
                  pl.BlockSpec((FF_TILE, D), lambda i, j: (nf - 1, 0)),
                  pl.BlockSpec((tm, D), lambda i, j: (i, 0))],
        out_specs=pl.BlockSpec((tm, D), lambda i, j: (i, 0)),
        out_shape=jax.ShapeDtypeStruct((T, D), F32),
        scratch_shapes=[u_scr, u_scr, u_scr, u_scr, a_scr, a_scr, pltpu.VMEM((tm, D), F32)],
        compiler_params=_cparams("parallel", "arbitrary"),
        name="conv_ffn",
    )(hn, hn, wb, wb, wb, wb, conv_w, conv_w, cb, cb, wd, wd, x1)


def kernel(x, mem, pos, mix_norm_g, w_in, ssd_conv_w, ssd_conv_b, ssd_dt_bias, ssd_a_log, ssd_d,
           ssd_norm_g, attn_q_norm_g, attn_k_norm_g, mem_norm_g, w_mem_kv, mem_q_norm_g,
           mem_k_norm_g, w_out, ffn_norm_g, w_up, ffn_conv_w, ffn_conv_b, w_down):
    B, L, D = x.shape
    T = B * L
    depth = w_in.shape[0]
    assert L % max(KEY_CHUNK, 1024 if L >= 1024 else KEY_CHUNK) == 0 and D == 1024
    xf = x.reshape(T, D)
    cos_t, sin_t = _rope_tables(pos)
    for l in range(depth):
        slab = _in_proj(xf, mix_norm_g[l], _slab_weight(w_in, l))
        y_ssd = _ssd(slab, B, L, ssd_conv_w[l], ssd_conv_b[l], ssd_dt_bias[l], ssd_a_log[l],
                     ssd_d[l], ssd_norm_g[l])
        qn, kn, vt, iqr, ika, ikb, iwt, mqn = _prep(slab, cos_t, sin_t, attn_q_norm_g[l],
                                                    attn_k_norm_g[l], mem_q_norm_g[l])
        mk, mvt = _mem_kv(mem, mem_norm_g[l], w_mem_kv[l], mem_k_norm_g[l])
        y_att = _attention(qn, iqr, iwt, mqn, ika, ikb, kn, vt, mk, mvt, B, L)
        x1, hn = _out_proj(xf, y_ssd, y_att, w_out[l], ffn_norm_g[l])
        xf = _ffn(x1, hn, w_up[l], ffn_conv_w[l], ffn_conv_b[l], w_down[l], L)
    return xf.reshape(B, L, D)
```

```python
import functools

import numpy as np
import jax
import jax.numpy as jnp
from jax import lax
from jax.experimental import pallas as pl
from jax.experimental.pallas import tpu as pltpu

F32 = jnp.float32
BF16 = jnp.bfloat16
I32 = jnp.int32
I16 = jnp.int16

EPS = 1e-6
ROPE_THETA = 500000.0
HEAD_DIM = 64
ROT_HALF = 8
SSD_HEADS = 8
SSD_INNER = 512
SSD_GROUPS = 2
SSD_STATE = 128
SSD_CONV = 4
SSD_CHUNK = 128
SSD_CONV_DIM = 1024
ATTN_INNER = 256
IDX_HEADS = 4
IDX_DIM = 64
TOPK_MAX = 256
Q_BLOCK = 512
N_MEM = 256
MEM_INNER = 256
D_FF = 2816
FFN_CONV = 3

SLAB_W = 2944
SMALL_DT = 64
SMALL_IW = 72
BLK_XBC, BLK_Z, BLK_Q, BLK_K, BLK_V, BLK_IQ, BLK_MQ, BLK_SMALL = 0, 2, 6, 7, 8, 9, 10, 22

KEY_CHUNK = 512
COUNT_LANES = 4
TIE_BLOCK = 256
FF_TILE = 256
FFN_ROWS = 64
FFN_SPLIT = 1
NEG_BIG = -1e30
LOG2_E = 1.4426950408889634
INT_MIN = -2 ** 31
KEY_NEG_INF = int(np.array(0xFF800000 ^ 0x7FFFFFFF, dtype=np.uint32).astype(np.int32))

VMEM_LIMIT = 56 * 1024 * 1024


def _cparams(*sem):
    return pltpu.CompilerParams(dimension_semantics=sem, vmem_limit_bytes=VMEM_LIMIT)


def _nt(a, b):
    return lax.dot_general(a, b, (((1,), (1,)), ((), ())), preferred_element_type=F32)


def _mm(a, b):
    return jnp.dot(a, b, preferred_element_type=F32)


def _sigmoid(x):
    return 1.0 / (1.0 + jnp.exp(-x))


def _silu(x):
    return x * _sigmoid(x)


def _softplus(x):
    return jnp.maximum(x, 0.0) + jnp.log(1.0 + jnp.exp(-jnp.abs(x)))


def _rope_kernel(pos_ref, invf_ref, sgn_ref, cos_ref, sin_ref):
    ang = pos_ref[...].astype(F32) * invf_ref[...]
    cos_ref[...] = jnp.cos(ang)
    sin_ref[...] = jnp.sin(ang) * sgn_ref[...]


def _rope_tables(pos):
    T = pos.size
    half = ROT_HALF
    inv_freq = ROPE_THETA ** (-(jnp.arange(half, dtype=F32) * 2.0 / (2 * half)))
    lane = np.arange(128) % HEAD_DIM
    rot = lane < 2 * half
    invf = jnp.where(jnp.asarray(rot), inv_freq[jnp.asarray(lane % half)], 0.0).reshape(1, 128)
    sgn = jnp.asarray(np.where(lane < half, -1.0, np.where(rot, 1.0, 0.0)), F32).reshape(1, 128)
    posb = jnp.broadcast_to(pos.reshape(T, 1), (T, 128))
    tm = min(T, 2048)
    return pl.pallas_call(
        _rope_kernel,
        grid=(T // tm,),
        in_specs=[pl.BlockSpec((tm, 128), lambda i: (i, 0)),
                  pl.BlockSpec((1, 128), lambda i: (0, 0)),
                  pl.BlockSpec((1, 128), lambda i: (0, 0))],
        out_specs=[pl.BlockSpec((tm, 128), lambda i: (i, 0)),
                   pl.BlockSpec((tm, 128), lambda i: (i, 0))],
        out_shape=[jax.ShapeDtypeStruct((T, 128), F32)] * 2,
        compiler_params=_cparams("parallel"),
        name="rope_tables",
    )(posb, invf, sgn)


def _rope(x, cos, sin_s):
    w = x.shape[-1]
    if w == 256:
        cos = jnp.concatenate([cos, cos], axis=1)
        sin_s = jnp.concatenate([sin_s, sin_s], axis=1)
    lane = lax.broadcasted_iota(I32, x.shape, 1)
    first = (lane & (HEAD_DIM - 1)) < ROT_HALF
    partner = jnp.where(first, pltpu.roll(x, w - ROT_HALF, 1), pltpu.roll(x, ROT_HALF, 1))
    return x * cos + partner * sin_s


def _seg_ones(n):
    r = lax.broadcasted_iota(I32, (n, n), 0) // HEAD_DIM
    c = lax.broadcasted_iota(I32, (n, n), 1) // HEAD_DIM
    return jnp.where(r == c, 1.0, 0.0).astype(BF16)


def _head_rms(x, seg):
    sq = x * x
    hi = sq.astype(BF16)
    lo = (sq - hi.astype(F32)).astype(BF16)
    ms = (_mm(hi, seg) + _mm(lo, seg)) * (1.0 / HEAD_DIM)
    return x * lax.rsqrt(ms + EPS)


def _in_proj_kernel(x_ref, g_ref, w_ref, o_ref):
    x = x_ref[...]
    r = lax.rsqrt(jnp.mean(x * x, axis=-1, keepdims=True) + EPS)
    h = ((x * r) * g_ref[...]).astype(BF16)
    o_ref[...] = _mm(h, w_ref[...])


def _in_proj(xf, g, w_slab):
    T, D = xf.shape
    tm = min(T, 1024)
    return pl.pallas_call(
        _in_proj_kernel,
        grid=(T // tm,),
        in_specs=[pl.BlockSpec((tm, D), lambda i: (i, 0)),
                  pl.BlockSpec((1, D), lambda i: (0, 0)),
                  pl.BlockSpec((D, SLAB_W), lambda i: (0, 0))],
        out_specs=pl.BlockSpec((tm, SLAB_W), lambda i: (i, 0)),
        out_shape=jax.ShapeDtypeStruct((T, SLAB_W), F32),
        compiler_params=_cparams("parallel"),
        name="in_proj",
    )(xf, g.reshape(1, D), w_slab)


def _slab_weight_kernel(w_ref, o_ref):
    rows = 256
    d = w_ref.shape[1]

    def body(r, _):
        rs = pl.ds(pl.multiple_of(r * rows, rows), rows)
        o_ref[rs, 0:1024] = w_ref[0, rs, 512:1536].astype(BF16)
        o_ref[rs, 1024:1536] = w_ref[0, rs, 0:512].astype(BF16)
        o_ref[rs, 1536:2560] = w_ref[0, rs, 1544:2568].astype(BF16)
        o_ref[rs, 2560:2816] = w_ref[0, rs, 2636:2892].astype(BF16)
        small = jnp.concatenate([w_ref[0, rs, 2568:2632], w_ref[0, rs, 1536:1544], w_ref[0, rs, 2632:2636],
                                 jnp.zeros((rows, SLAB_W - 2892), F32)], axis=1)
        o_ref[rs, 2816:SLAB_W] = small.astype(BF16)
        return 0

    lax.fori_loop(0, d // rows, body, 0)


def _slab_weight(w_in, l):
    _, d, n = w_in.shape
    return pl.pallas_call(
        _slab_weight_kernel,
        grid=(1,),
        in_specs=[pl.BlockSpec((1, d, n), lambda i: (l, 0, 0))],
        out_specs=pl.BlockSpec((d, SLAB_W), lambda i: (0, 0)),
        out_shape=jax.ShapeDtypeStruct((d, SLAB_W), BF16),
        compiler_params=_cparams("arbitrary"),
        name="slab_weight",
    )(w_in)


def _ssd_kernel(xbc_ref, z_ref, sm_ref, cw_ref, cb_ref, dtb_ref, alog_ref, dsk_ref, ng_ref,
                o_ref, st_ref, carry_ref, *, ts):
    q = SSD_CHUNK
    p = HEAD_DIM

    @pl.when(pl.program_id(1) == 0)
    def _():
        st_ref[...] = jnp.zeros_like(st_ref)
        carry_ref[...] = jnp.zeros_like(carry_ref)

    cw = cw_ref[...]
    cb = cb_ref[...]
    dtb = dtb_ref[...]
    a_row = -jnp.exp(alog_ref[...])
    dsk = dsk_ref[...]
    ng = ng_ref[...]
    row8 = lax.broadcasted_iota(I32, (8, SSD_CONV_DIM), 0)
    rowi = lax.broadcasted_iota(I32, (q, q), 0)
    coli = lax.broadcasted_iota(I32, (q, q), 1)
    tril = rowi >= coli

    def chunk(c, _):
        off = pl.multiple_of(c * q, q)
        cur = xbc_ref[pl.ds(off, q), :]
        poff = pl.multiple_of(jnp.maximum(off - 8, 0), 8)
        prev8 = jnp.where(c == 0, carry_ref[...], xbc_ref[pl.ds(poff, 8), :])
        acc = cur * cw[SSD_CONV - 1:SSD_CONV, :] + cb
        for k in range(1, SSD_CONV):
            rk = pltpu.roll(cur, k, 0)
            pk = pltpu.roll(prev8, k, 0)
            top = jnp.where(row8 < k, pk, rk[0:8, :])
            sh = jnp.concatenate([top, rk[8:, :]], axis=0)
            acc = acc + sh * cw[SSD_CONV - 1 - k:SSD_CONV - k, :]
        xc = _silu(acc)
        xs = xc[:, :SSD_INNER]

        sm = sm_ref[pl.ds(off, q), :]
        dtv = _softplus(sm + dtb)
        cs = dtv * a_row
        s = 1
        while s < q:
            cs = cs + jnp.where(rowi >= s, pltpu.roll(cs, s, 0), 0.0)
            s *= 2
        cst = cs.T

        ys = []
        for g in range(SSD_GROUPS):
            b0 = SSD_INNER + g * SSD_STATE
            c0 = SSD_INNER + SSD_GROUPS * SSD_STATE + g * SSD_STATE
            bm = xc[:, b0:b0 + SSD_STATE]
            cmb = xc[:, c0:c0 + SSD_STATE].astype(BF16)
            cbm = _nt(cmb, bm.astype(BF16))
            bmt = bm.T.astype(BF16)
            for r in range(SSD_HEADS // SSD_GROUPS):
                h = g * (SSD_HEADS // SSD_GROUPS) + r
                hl = SMALL_DT + h
                colb = jnp.broadcast_to(cs[:, hl:hl + 1], (q, q))
                seg = colb - cst[hl:hl + 1, :]
                lm = jnp.exp(jnp.where(tril, seg, -jnp.inf))
                xs_h = xs[:, h * p:(h + 1) * p]
                xdt = xs_h * jnp.broadcast_to(dtv[:, hl:hl + 1], (q, p))
                yd = _mm((cbm * lm).astype(BF16), xdt.astype(BF16))
                cb64 = colb[:, :p]
                last = jnp.broadcast_to(cb64[q - 1:q, :], (q, p))
                xw = (xdt * jnp.exp(last - cb64)).astype(BF16)
                st_new = _mm(bmt, xw)
                st_prev = st_ref[h]
                yo = _mm(cmb, st_prev.astype(BF16)) * jnp.exp(cb64)
                st_ref[h] = st_prev * jnp.exp(last) + st_new
                ys.append(yd + yo + xs_h * dsk[:, h * p:(h + 1) * p])
        y = jnp.concatenate(ys, axis=1)
        yg = y * _silu(z_ref[pl.ds(off, q), :])
        gw = SSD_INNER // SSD_GROUPS
        outs = []
        for g in range(SSD_GROUPS):
            v = yg[:, g * gw:(g + 1) * gw]
            rr = lax.rsqrt(jnp.mean(v * v, axis=-1, keepdims=True) + EPS)
            outs.append((v * rr) * ng[:, g * gw:(g + 1) * gw])
        o_ref[pl.ds(off, q), :] = jnp.concatenate(outs, axis=1).astype(o_ref.dtype)
        return 0

    lax.fori_loop(0, ts // q, chunk, 0)
    carry_ref[...] = xbc_ref[ts - 8:ts, :]


def _ssd(slab, B, L, conv_w, conv_b, dt_bias, a_log, d_skip, norm_g):
    T = B * L
    ts = min(L, 1024)
    nj = L // ts
    dtb = jnp.zeros((1, 128), F32).at[0, SMALL_DT:SMALL_DT + SSD_HEADS].set(dt_bias)
    alog = jnp.zeros((1, 128), F32).at[0, SMALL_DT:SMALL_DT + SSD_HEADS].set(a_log)
    dsk = jnp.repeat(d_skip, HEAD_DIM).reshape(1, SSD_INNER)
    row = lambda w: pl.BlockSpec((1, w), lambda b, j: (0, 0))
    return pl.pallas_call(
        functools.partial(_ssd_kernel, ts=ts),
        grid=(B, nj),
        in_specs=[pl.BlockSpec((ts, SSD_CONV_DIM), lambda b, j: (b * nj + j, BLK_XBC)),
                  pl.BlockSpec((ts, SSD_INNER), lambda b, j: (b * nj + j, BLK_Z)),
                  pl.BlockSpec((ts, 128), lambda b, j: (b * nj + j, BLK_SMALL)),
                  pl.BlockSpec((SSD_CONV, SSD_CONV_DIM), lambda b, j: (0, 0)),
                  row(SSD_CONV_DIM), row(128), row(128), row(SSD_INNER), row(SSD_INNER)],
        out_specs=pl.BlockSpec((ts, SSD_INNER), lambda b, j: (b * nj + j, 0)),
        out_shape=jax.ShapeDtypeStruct((T, SSD_INNER), BF16),
        scratch_shapes=[pltpu.VMEM((SSD_HEADS, SSD_STATE, HEAD_DIM), F32),
                        pltpu.VMEM((8, SSD_CONV_DIM), F32)],
        compiler_params=_cparams("arbitrary", "arbitrary"),
        name="ssd",
    )(slab, slab, slab, conv_w, conv_b.reshape(1, -1), dtb, alog, dsk, norm_g.reshape(1, -1))


def _prep_kernel(q_ref, k_ref, v_ref, iq_ref, mq_ref, sm_ref, cos_ref, sin_ref,
                 qg_ref, kg_ref, mg_ref,
                 qn_ref, kn_ref, vt_ref, iqr_ref, ika_ref, ikb_ref, iwt_ref, mqn_ref):
    cos = cos_ref[...]
    sin_s = sin_ref[...]
    seg = _seg_ones(ATTN_INNER)
    scale = (HEAD_DIM ** -0.5) * LOG2_E
    qn = _rope(_head_rms(q_ref[...], seg) * qg_ref[...], cos, sin_s) * scale
    qn_ref[...] = qn.astype(BF16)
    kn = _rope(_head_rms(k_ref[...], seg) * kg_ref[...], cos, sin_s)
    kn_ref[...] = kn.astype(BF16)
    vt_ref[0] = v_ref[...].T.astype(BF16)
    iqr_ref[...] = _rope(iq_ref[...], cos, sin_s).astype(BF16)
    mqn_ref[...] = ((_head_rms(mq_ref[...], seg) * mg_ref[...]) * scale).astype(BF16)
    sm = sm_ref[...]
    ikr = _rope(sm, cos, sin_s)
    lane = lax.broadcasted_iota(I32, ikr.shape, 1)
    ika = jnp.where(lane < IDX_DIM, ikr, 0.0)
    ika_ref[...] = ika.astype(BF16)
    ikb_ref[...] = pltpu.roll(ika, IDX_DIM, 1).astype(BF16)
    smt = sm.T
    iwt_ref[...] = smt[SMALL_IW:SMALL_IW + 8, :] * ((IDX_HEADS ** -0.5) * (IDX_DIM ** -0.5))


def _prep(slab, cos_t, sin_t, qg, kg, mg):
    T = slab.shape[0]
    tm = KEY_CHUNK
    tile4 = lambda g: jnp.tile(g, ATTN_INNER // HEAD_DIM).reshape(1, ATTN_INNER)
    col = lambda w, blk: pl.BlockSpec((tm, w), lambda i: (i, blk))
    row = lambda w: pl.BlockSpec((1, w), lambda i: (0, 0))
    out = lambda w: pl.BlockSpec((tm, w), lambda i: (i, 0))
    return pl.pallas_call(
        _prep_kernel,
        grid=(T // tm,),
        in_specs=[col(256, BLK_Q), col(256, BLK_K), col(256, BLK_V), col(256, BLK_IQ),
                  col(256, BLK_MQ), col(128, BLK_SMALL), out(128), out(128),
                  row(256), row(256), row(256)],
        out_specs=[out(256), out(256), pl.BlockSpec((1, 256, tm), lambda i: (i, 0, 0)),
                   out(256), out(128), out(128), pl.BlockSpec((8, tm), lambda i: (0, i)), out(256)],
        out_shape=[jax.ShapeDtypeStruct((T, 256), BF16), jax.ShapeDtypeStruct((T, 256), BF16),
                   jax.ShapeDtypeStruct((T // tm, 256, tm), BF16),
                   jax.ShapeDtypeStruct((T, 256), BF16), jax.ShapeDtypeStruct((T, 128), BF16),
                   jax.ShapeDtypeStruct((T, 128), BF16), jax.ShapeDtypeStruct((8, T), F32),
                   jax.ShapeDtypeStruct((T, 256), BF16)],
        compiler_params=_cparams("parallel"),
        name="attn_prep",
    )(slab, slab, slab, slab, slab, slab, cos_t, sin_t, tile4(qg), tile4(kg), tile4(mg))


def _mem_kv_kernel(mem_ref, g_ref, w_ref, kg_ref, mk_ref, mvt_ref):
    x = mem_ref[...]
    r = lax.rsqrt(jnp.mean(x * x, axis=-1, keepdims=True) + EPS)
    h = ((x * r) * g_ref[...]).astype(BF16)
    kv = _mm(h, w_ref[...])
    seg = _seg_ones(MEM_INNER)
    mk_ref[...] = (_head_rms(kv[:, :MEM_INNER], seg) * kg_ref[...]).astype(BF16)
    mvt_ref[0] = kv[:, MEM_INNER:].T.astype(BF16)


def _mem_kv(mem, g, w, kg):
    B, M, D = mem.shape
    return pl.pallas_call(
        _mem_kv_kernel,
        grid=(B,),
        in_specs=[pl.BlockSpec((M, D), lambda b: (b, 0)),
                  pl.BlockSpec((1, D), lambda b: (0, 0)),
                  pl.BlockSpec((D, 2 * MEM_INNER), lambda b: (0, 0)),
                  pl.BlockSpec((1, MEM_INNER), lambda b: (0, 0))],
        out_specs=[pl.BlockSpec((M, MEM_INNER), lambda b: (b, 0)),
                   pl.BlockSpec((1, MEM_INNER, M), lambda b: (b, 0, 0))],
        out_shape=[jax.ShapeDtypeStruct((B * M, MEM_INNER), BF16),
                   jax.ShapeDtypeStruct((B, MEM_INNER, M), BF16)],
        compiler_params=_cparams("parallel"),
        name="mem_kv",
    )(mem.reshape(B * M, D), g.reshape(1, D), w.astype(BF16),
      jnp.tile(kg, MEM_INNER // HEAD_DIM).reshape(1, MEM_INNER))


def _split_heads(x):
    lane = lax.broadcasted_iota(I32, (Q_BLOCK, 128), 1)
    lo = lane < HEAD_DIM
    zero = jnp.zeros((Q_BLOCK, 128), x.dtype)
    heads = []
    for pr in range(2):
        xp = x[:, pr * 128:(pr + 1) * 128]
        heads += [jnp.where(lo, xp, zero), jnp.where(lo, zero, xp)]
    return heads


def _attn_kernel(qn_ref, iqr_ref, iwt_ref, mqn_ref, ika_ref, ikb_ref, kn_ref, vt_ref, mk_ref, mvt_ref,
                 o_ref, keys_ref, khi_ref, klo_ref, bias_ref, sm_ref, pb_ref, *, topk):
    kc = KEY_CHUNK
    qb = Q_BLOCK
    nh = ATTN_INNER // HEAD_DIM
    i = pl.program_id(1)
    q0 = i * qb
    nch = (q0 + qb + kc - 1) // kc
    iw = iwt_ref[...]
    iq = iqr_ref[...]
    rel = (lax.broadcasted_iota(I32, (kc, qb), 0) - lax.broadcasted_iota(I32, (kc, qb), 1)) - q0

    def chunk_off(c):
        return pl.multiple_of(c * kc, kc)

    def p1(c, _):
        off = chunk_off(c)
        ika = ika_ref[pl.ds(off, kc), :]
        ikb = ikb_ref[pl.ds(off, kc), :]
        isc = iw[0:1, :] * jnp.maximum(_nt(ika, iq[:, :128]), 0.0)
        isc = isc + iw[1:2, :] * jnp.maximum(_nt(ikb, iq[:, :128]), 0.0)
        isc = isc + iw[2:3, :] * jnp.maximum(_nt(ika, iq[:, 128:]), 0.0)
        isc = isc + iw[3:4, :] * jnp.maximum(_nt(ikb, iq[:, 128:]), 0.0)
        isc = jnp.where(rel <= -off, isc, -jnp.inf)
        bits = pltpu.bitcast(isc, I32)
        ks = jnp.where(bits < 0, bits ^ 0x7FFFFFFF, bits)
        keys_ref[pl.ds(off, kc), :] = ks
        khi_ref[pl.ds(off, kc), :] = (ks >> 16).astype(I16)
        klo_ref[pl.ds(off, kc), :] = (ks ^ 0x8000).astype(I16)
        return 0

    lax.fori_loop(0, nch, p1, 0)

    one_b = jnp.ones((16, qb), BF16)
    zero_b = jnp.zeros((16, qb), BF16)

    def count16(ref, cand, strict):
        def body(c, acc):
            off = chunk_off(c)
            parts = [zero_b] * COUNT_LANES
            for r in range(kc // 16):
                kk = ref[pl.ds(off + r * 16, 16), :]
                m = jnp.where(kk > cand if strict else kk >= cand, one_b, zero_b)
                parts[r % COUNT_LANES] = parts[r % COUNT_LANES] + m
            while len(parts) > 1:
                parts = [parts[r] + parts[r + 1] for r in range(0, len(parts), 2)]
            return acc + parts[0].astype(F32)
        acc = lax.fori_loop(0, nch, body, jnp.zeros((16, qb), F32))
        return jnp.sum(acc, axis=0, keepdims=True)

    def kth_largest16(ref, need):
        def bis(it, t_u):
            cand_u = t_u | jnp.left_shift(jnp.int32(1), 15 - it)
            cnt = count16(ref, (cand_u - 32768).astype(I16), False)
            return jnp.where(cnt >= need, cand_u, t_u)
        return lax.fori_loop(0, 16, bis, jnp.zeros((1, qb), I32)) - 32768

    t_hi = kth_largest16(khi_ref, float(topk))
    t_hi16 = t_hi.astype(I16)
    gt_hi = count16(khi_ref, t_hi16, True)

    def band(c, _):
        off = chunk_off(c)
        klo_ref[pl.ds(off, kc), :] = jnp.where(khi_ref[pl.ds(off, kc), :] == t_hi16,
                                               klo_ref[pl.ds(off, kc), :], jnp.int16(-32768))
        return 0

    lax.fori_loop(0, nch, band, 0)
    t_lo = kth_largest16(klo_ref, topk - gt_hi)
    gt_lo = count16(klo_ref, t_lo.astype(I16), True)
    t = t_hi * 65536 + (t_lo + 32768)
    r_eq = jnp.where(t == KEY_NEG_INF, 0.0, topk - (gt_hi + gt_lo))

    q_heads = _split_heads(qn_ref[...])
    tb = TIE_BLOCK
    tri = jnp.where(lax.broadcasted_iota(I32, (tb, tb), 0) >= lax.broadcasted_iota(I32, (tb, tb), 1),
                    1.0, 0.0).astype(BF16)

    def p3(c, carry):
        eq_cnt, ms, ls, accs = carry
        off = chunk_off(c)
        for r in range(kc // tb):
            k = keys_ref[pl.ds(off + r * tb, tb), :]
            eq = k == t
            pref = _mm(tri, jnp.where(eq, 1.0, 0.0).astype(BF16)) + eq_cnt
            eq_cnt = pref[tb - 1:tb, :]
            sel = (k > t) | (eq & (pref <= r_eq))
            bias_ref[r * tb:(r + 1) * tb, :] = jnp.where(sel, 0.0, -jnp.inf)
        kn = kn_ref[pl.ds(off, kc), :]
        m_news, alphas = [], []
        for h in range(nh):
            s = _nt(kn[:, (h // 2) * 128:(h // 2 + 1) * 128], q_heads[h]) + bias_ref[...]
            sm_ref[h] = s
            cm = jnp.max(jnp.max(s.reshape(kc // 8, 8, qb), axis=0), axis=0, keepdims=True)
            m_new = jnp.maximum(ms[h], cm)
            m_news.append(m_new)
            alphas.append(jnp.exp2(ms[h] - m_new))
        ls2 = []
        for h in range(nh):
            pexp = jnp.exp2(sm_ref[h] - m_news[h])
            psum = jnp.sum(jnp.sum(pexp.reshape(kc // 8, 8, qb), axis=0), axis=0, keepdims=True)
            ls2.append(ls[h] * alphas[h] + psum)
            pb_ref[h] = pexp.astype(BF16)
        vt = vt_ref[c]
        accs2 = [accs[h] * alphas[h] + _mm(vt[h * HEAD_DIM:(h + 1) * HEAD_DIM, :], pb_ref[h])
                 for h in range(nh)]
        return eq_cnt, tuple(m_news), tuple(ls2), tuple(accs2)

    init = (jnp.zeros((1, qb), F32),
            tuple(jnp.full((1, qb), NEG_BIG, F32) for _ in range(nh)),
            tuple(jnp.zeros((1, qb), F32) for _ in range(nh)),
            tuple(jnp.zeros((HEAD_DIM, qb), F32) for _ in range(nh)))
    _, _, ls, accs = lax.fori_loop(0, nch, p3, init)
    outs = [accs[h] * (1.0 / ls[h]) for h in range(nh)]

    m_heads = _split_heads(mqn_ref[...])
    mk = mk_ref[...]
    mvt = mvt_ref[0]
    for h in range(MEM_INNER // HEAD_DIM):
        s = _nt(mk[:, (h // 2) * 128:(h // 2 + 1) * 128], m_heads[h])
        mx = jnp.max(s, axis=0, keepdims=True)
        pexp = jnp.exp2(s - mx)
        den = jnp.sum(pexp, axis=0, keepdims=True)
        o = _mm(mvt[h * HEAD_DIM:(h + 1) * HEAD_DIM, :], pexp.astype(BF16))
        outs.append(o * (1.0 / den))
    o_ref[...] = jnp.concatenate(outs, axis=0).T.astype(o_ref.dtype)


def _attention(qn, iqr, iwt, mqn, ika, ikb, kn, vt, mk, mvt, B, L):
    T = B * L
    nq = L // Q_BLOCK
    ncl = L // KEY_CHUNK
    topk = min(TOPK_MAX, L // 4)
    assert KEY_CHUNK >= topk and KEY_CHUNK % TIE_BLOCK == 0
    nh = ATTN_INNER // HEAD_DIM
    qblk = lambda w: pl.BlockSpec((Q_BLOCK, w), lambda b, i: (b * nq + i, 0))
    kblk = lambda w: pl.BlockSpec((L, w), lambda b, i: (b, 0))
    return pl.pallas_call(
        functools.partial(_attn_kernel, topk=topk),
        grid=(B, nq),
        in_specs=[qblk(256), qblk(256), pl.BlockSpec((8, Q_BLOCK), lambda b, i: (0, b * nq + i)), qblk(256),
                  kblk(128), kblk(128), kblk(256),
                  pl.BlockSpec((ncl, 256, KEY_CHUNK), lambda b, i: (b, 0, 0)),
                  pl.BlockSpec((N_MEM, MEM_INNER), lambda b, i: (b, 0)),
                  pl.BlockSpec((1, MEM_INNER, N_MEM), lambda b, i: (b, 0, 0))],
        out_specs=pl.BlockSpec((Q_BLOCK, ATTN_INNER + MEM_INNER), lambda b, i: (b * nq + i, 0)),
        out_shape=jax.ShapeDtypeStruct((T, ATTN_INNER + MEM_INNER), BF16),
        scratch_shapes=[pltpu.VMEM((L, Q_BLOCK), I32), pltpu.VMEM((L, Q_BLOCK), I16),
                        pltpu.VMEM((L, Q_BLOCK), I16), pltpu.VMEM((KEY_CHUNK, Q_BLOCK), F32),
                        pltpu.VMEM((nh, KEY_CHUNK, Q_BLOCK), F32),
                        pltpu.VMEM((nh, KEY_CHUNK, Q_BLOCK), BF16)],
        compiler_params=_cparams("parallel", "arbitrary"),
        name="attention",
    )(qn, iqr, iwt, mqn, ika, ikb, kn, vt, mk, mvt)


def _out_proj_kernel(x_ref, ys_ref, ya_ref, w_ref, g_ref, x1_ref, hn_ref):
    w = w_ref[...]
    x1 = x_ref[...] + _mm(ys_ref[...], w[:SSD_INNER, :]) + _mm(ya_ref[...], w[SSD_INNER:, :])
    x1_ref[...] = x1
    r = lax.rsqrt(jnp.mean(x1 * x1, axis=-1, keepdims=True) + EPS)
    hn_ref[...] = ((x1 * r) * g_ref[...]).astype(BF16)


def _out_proj(xf, y_ssd, y_att, w_out, l, g):
    T, D = xf.shape
    tm = min(T, 1024)
    blk = lambda w: pl.BlockSpec((tm, w), lambda i: (i, 0))
    return pl.pallas_call(
        _out_proj_kernel,
        grid=(T // tm,),
        in_specs=[blk(D), blk(SSD_INNER), blk(ATTN_INNER + MEM_INNER),
                  pl.BlockSpec((None, D, D), lambda i: (l, 0, 0)), pl.BlockSpec((1, D), lambda i: (0, 0))],
        out_specs=[blk(D), blk(D)],
        out_shape=[jax.ShapeDtypeStruct((T, D), F32), jax.ShapeDtypeStruct((T, D), BF16)],
        compiler_params=_cparams("parallel"),
        name="out_proj",
    )(xf, y_ssd, y_att, w_out, g.reshape(1, D))


def _ffn_kernel(h_ref, halo_ref, wg0_ref, wv0_ref, wgn_ref, wvn_ref, cwg_ref, cwv_ref, cbg_ref, cbv_ref,
                wdp_ref, wdl_ref, x1_ref, o_ref,
                uga_ref, uva_ref, ugb_ref, uvb_ref, aa_ref, ab_ref, acc_ref, *, tiles_per_seq, nf):
    i = pl.program_id(0)
    j = pl.program_id(1)
    halo = halo_ref[...]
    halo = jnp.where(i % tiles_per_seq == 0, jnp.zeros_like(halo), halo)
    tm = h_ref.shape[0]
    rs = FFN_ROWS
    cwg = cwg_ref[...]
    cwv = cwv_ref[...]
    cbg = cbg_ref[...]
    cbv = cbv_ref[...]

    def up(w, u_ref, r0, nrows):
        if r0 == 0:
            u_ref[0:8, :] = _mm(halo, w)
        u_ref[8 + r0:8 + r0 + nrows, :] = _mm(h_ref[r0:r0 + nrows, :], w)

    def conv(u_ref, r0, cw, cb):
        blk = u_ref[r0:r0 + rs + 8, :]
        acc = blk[8:, :] * cw[FFN_CONV - 1:FFN_CONV, :] + cb
        for k in range(1, FFN_CONV):
            acc = acc + blk[8 - k:8 - k + rs, :] * cw[FFN_CONV - 1 - k:FFN_CONV - k, :]
        return acc

    @pl.when(j == 0)
    def _():
        up(wg0_ref[...], uga_ref, 0, tm)
        up(wv0_ref[...], uva_ref, 0, tm)
        ab_ref[...] = jnp.zeros_like(ab_ref)
        acc_ref[...] = jnp.zeros_like(acc_ref)

    def stage(u_cur, u_nxt, a_cur, a_prv):
        qr = tm // FFN_SPLIT
        wgn = wgn_ref[...]
        wvn = wvn_ref[...]
        wdp = wdp_ref[...]
        for q in range(FFN_SPLIT):
            up(wgn, u_nxt[0], q * qr, qr)
            up(wvn, u_nxt[1], q * qr, qr)
            for r0 in range(q * qr, (q + 1) * qr, rs):
                gate = conv(u_cur[0], r0, cwg, cbg)
                val = conv(u_cur[1], r0, cwv, cbv)
                a_cur[r0:r0 + rs, :] = (_silu(gate) * val).astype(BF16)
            acc_ref[q * qr:(q + 1) * qr, :] += _mm(a_prv[q * qr:(q + 1) * qr, :], wdp)

    @pl.when(j % 2 == 0)
    def _():
        stage((uga_ref, uva_ref), (ugb_ref, uvb_ref), aa_ref, ab_ref)

    @pl.when(j % 2 == 1)
    def _():
        stage((ugb_ref, uvb_ref), (uga_ref, uva_ref), ab_ref, aa_ref)

    @pl.when(j == nf - 1)
    def _():
        a_last = aa_ref if (nf - 1) % 2 == 0 else ab_ref
        o_ref[...] = x1_ref[...] + (acc_ref[...] + _mm(a_last[...], wdl_ref[...]))


def _ffn(x1, hn, wb, conv_w, conv_b, wd, l, L):
    T, D = x1.shape
    tm = min(L, 1024)
    nf = D_FF // FF_TILE
    cb = conv_b.reshape(1, 2 * D_FF)
    hb = tm // 8
    nxt = lambda j: jnp.minimum(j + 1, nf - 1)
    wspec = lambda f: pl.BlockSpec((None, D, FF_TILE), f)
    u_scr = pltpu.VMEM((tm + 8, FF_TILE), F32)
    a_scr = pltpu.VMEM((tm, FF_TILE), BF16)
    return pl.pallas_call(
        functools.partial(_ffn_kernel, tiles_per_seq=L // tm, nf=nf),
        grid=(T // tm, nf),
        in_specs=[pl.BlockSpec((tm, D), lambda i, j: (i, 0)),
                  pl.BlockSpec((8, D), lambda i, j: (jnp.maximum(i * hb - 1, 0), 0)),
                  wspec(lambda i, j: (l, 0, 0)), wspec(lambda i, j: (l, 0, nf)),
                  wspec(lambda i, j: (l, 0, nxt(j))), wspec(lambda i, j: (l, 0, nf + nxt(j))),
                  pl.BlockSpec((FFN_CONV, FF_TILE), lambda i, j: (0, j)),
                  pl.BlockSpec((FFN_CONV, FF_TILE), lambda i, j: (0, nf + j)),
                  pl.BlockSpec((1, FF_TILE), lambda i, j: (0, j)),
                  pl.BlockSpec((1, FF_TILE), lambda i, j: (0, nf + j)),
                  pl.BlockSpec((None, FF_TILE, D), lambda i, j: (l, jnp.maximum(j - 1, 0), 0)),
                  pl.BlockSpec((None, FF_TILE, D), lambda i, j: (l, nf - 1, 0)),
                  pl.BlockSpec((tm, D), lambda i, j: (i, 0))],
        out_specs=pl.BlockSpec((tm, D), lambda i, j: (i, 0)),
        out_shape=jax.ShapeDtypeStruct((T, D), F32),
        scratch_shapes=[u_scr, u_scr, u_scr, u_scr, a_scr, a_scr, pltpu.VMEM((tm, D), F32)],
        compiler_params=_cparams("parallel", "arbitrary"),
        name="conv_ffn",
    )(hn, hn, wb, wb, wb, wb, conv_w, conv_w, cb, cb, wd, wd, x1)


def kernel(x, mem, pos, mix_norm_g, w_in, ssd_conv_w, ssd_conv_b, ssd_dt_bias, ssd_a_log, ssd_d,
           ssd_norm_g, attn_q_norm_g, attn_k_norm_g, mem_norm_g, w_mem_kv, mem_q_norm_g,
           mem_k_norm_g, w_out, ffn_norm_g, w_up, ffn_conv_w, ffn_conv_b, w_down):
    B, L, D = x.shape
    T = B * L
    depth = w_in.shape[0]
    assert L % max(KEY_CHUNK, 1024 if L >= 1024 else KEY_CHUNK) == 0 and D == 1024
    xf = x.reshape(T, D)
    cos_t, sin_t = _rope_tables(pos)
    w_out_b, w_up_b, w_down_b = w_out.astype(BF16), w_up.astype(BF16), w_down.astype(BF16)
    for l in range(depth):
        slab = _in_proj(xf, mix_norm_g[l], _slab_weight(w_in, l))
        y_ssd = _ssd(slab, B, L, ssd_conv_w[l], ssd_conv_b[l], ssd_dt_bias[l], ssd_a_log[l],
                     ssd_d[l], ssd_norm_g[l])
        qn, kn, vt, iqr, ika, ikb, iwt, mqn = _prep(slab, cos_t, sin_t, attn_q_norm_g[l],
                                                    attn_k_norm_g[l], mem_q_norm_g[l])
        mk, mvt = _mem_kv(mem, mem_norm_g[l], w_mem_kv[l], mem_k_norm_g[l])
        y_att = _attention(qn, iqr, iwt, mqn, ika, ikb, kn, vt, mk, mvt, B, L)
        x1, hn = _out_proj(xf, y_ssd, y_att, w_out_b, l, ffn_norm_g[l])
        xf = _ffn(x1, hn, w_up_b, ffn_conv_w[l], ffn_conv_b[l], w_down_b, l, L)
    return xf.reshape(B, L, D)
```

```python
import functools

import numpy as np
import jax
import jax.numpy as jnp
from jax import lax
from jax.experimental import pallas as pl
from jax.experimental.pallas import tpu as pltpu

F32 = jnp.float32
BF16 = jnp.bfloat16
I32 = jnp.int32
I16 = jnp.int16

EPS = 1e-6
ROPE_THETA = 500000.0
HEAD_DIM = 64
ROT_HALF = 8
SSD_HEADS = 8
SSD_INNER = 512
SSD_GROUPS = 2
SSD_STATE = 128
SSD_CONV = 4
SSD_CHUNK = 128
SSD_CONV_DIM = 1024
ATTN_INNER = 256
IDX_HEADS = 4
IDX_DIM = 64
TOPK_MAX = 256
Q_BLOCK = 512
N_MEM = 256
MEM_INNER = 256
D_FF = 2816
FFN_CONV = 3

SLAB_W = 2944
SMALL_DT = 64
SMALL_IW = 72
BLK_XBC, BLK_Z, BLK_Q, BLK_K, BLK_V, BLK_IQ, BLK_MQ, BLK_SMALL = 0, 2, 6, 7, 8, 9, 10, 22

KEY_CHUNK = 512
COUNT_LANES = 4
TIE_BLOCK = 256
FF_TILE = 256
FFN_ROWS = 64
FFN_SPLIT = 1
NEG_BIG = -1e30
LOG2_E = 1.4426950408889634
INT_MIN = -2 ** 31
KEY_NEG_INF = int(np.array(0xFF800000 ^ 0x7FFFFFFF, dtype=np.uint32).astype(np.int32))

VMEM_LIMIT = 56 * 1024 * 1024


def _cparams(*sem):
    return pltpu.CompilerParams(dimension_semantics=sem, vmem_limit_bytes=VMEM_LIMIT)


def _nt(a, b):
    return lax.dot_general(a, b, (((1,), (1,)), ((), ())), preferred_element_type=F32)


def _mm(a, b):
    return jnp.dot(a, b, preferred_element_type=F32)


def _sigmoid(x):
    return 1.0 / (1.0 + jnp.exp(-x))


def _silu(x):
    return x * _sigmoid(x)


def _softplus(x):
    return jnp.maximum(x, 0.0) + jnp.log(1.0 + jnp.exp(-jnp.abs(x)))


def _rope_kernel(pos_ref, invf_ref, sgn_ref, cos_ref, sin_ref):
    ang = pos_ref[...].astype(F32) * invf_ref[...]
    cos_ref[...] = jnp.cos(ang)
    sin_ref[...] = jnp.sin(ang) * sgn_ref[...]


def _rope_tables(pos):
    T = pos.size
    half = ROT_HALF
    inv_freq = ROPE_THETA ** (-(jnp.arange(half, dtype=F32) * 2.0 / (2 * half)))
    lane = np.arange(128) % HEAD_DIM
    rot = lane < 2 * half
    invf = jnp.where(jnp.asarray(rot), inv_freq[jnp.asarray(lane % half)], 0.0).reshape(1, 128)
    sgn = jnp.asarray(np.where(lane < half, -1.0, np.where(rot, 1.0, 0.0)), F32).reshape(1, 128)
    posb = jnp.broadcast_to(pos.reshape(T, 1), (T, 128))
    tm = min(T, 2048)
    return pl.pallas_call(
        _rope_kernel,
        grid=(T // tm,),
        in_specs=[pl.BlockSpec((tm, 128), lambda i: (i, 0)),
                  pl.BlockSpec((1, 128), lambda i: (0, 0)),
                  pl.BlockSpec((1, 128), lambda i: (0, 0))],
        out_specs=[pl.BlockSpec((tm, 128), lambda i: (i, 0)),
                   pl.BlockSpec((tm, 128), lambda i: (i, 0))],
        out_shape=[jax.ShapeDtypeStruct((T, 128), F32)] * 2,
        compiler_params=_cparams("parallel"),
        name="rope_tables",
    )(posb, invf, sgn)


def _rope(x, cos, sin_s):
    w = x.shape[-1]
    if w == 256:
        cos = jnp.concatenate([cos, cos], axis=1)
        sin_s = jnp.concatenate([sin_s, sin_s], axis=1)
    lane = lax.broadcasted_iota(I32, x.shape, 1)
    first = (lane & (HEAD_DIM - 1)) < ROT_HALF
    partner = jnp.where(first, pltpu.roll(x, w - ROT_HALF, 1), pltpu.roll(x, ROT_HALF, 1))
    return x * cos + partner * sin_s


def _seg_ones(n):
    r = lax.broadcasted_iota(I32, (n, n), 0) // HEAD_DIM
    c = lax.broadcasted_iota(I32, (n, n), 1) // HEAD_DIM
    return jnp.where(r == c, 1.0, 0.0).astype(BF16)


def _head_rms(x, seg):
    sq = x * x
    hi = sq.astype(BF16)
    lo = (sq - hi.astype(F32)).astype(BF16)
    ms = (_mm(hi, seg) + _mm(lo, seg)) * (1.0 / HEAD_DIM)
    return x * lax.rsqrt(ms + EPS)


def _in_proj_kernel(x_ref, g_ref, w_ref, o_ref):
    x = x_ref[...]
    r = lax.rsqrt(jnp.mean(x * x, axis=-1, keepdims=True) + EPS)
    h = ((x * r) * g_ref[...]).astype(BF16)
    o_ref[...] = _mm(h, w_ref[...])


def _in_proj(xf, g, w_slab):
    T, D = xf.shape
    tm = min(T, 1024)
    return pl.pallas_call(
        _in_proj_kernel,
        grid=(T // tm,),
        in_specs=[pl.BlockSpec((tm, D), lambda i: (i, 0)),
                  pl.BlockSpec((1, D), lambda i: (0, 0)),
                  pl.BlockSpec((D, SLAB_W), lambda i: (0, 0))],
        out_specs=pl.BlockSpec((tm, SLAB_W), lambda i: (i, 0)),
        out_shape=jax.ShapeDtypeStruct((T, SLAB_W), F32),
        compiler_params=_cparams("parallel"),
        name="in_proj",
    )(xf, g.reshape(1, D), w_slab)


def _slab_weight_kernel(w_ref, o_ref):
    rows = 256
    d = w_ref.shape[1]

    def body(r, _):
        rs = pl.ds(pl.multiple_of(r * rows, rows), rows)
        o_ref[rs, 0:1024] = w_ref[0, rs, 512:1536].astype(BF16)
        o_ref[rs, 1024:1536] = w_ref[0, rs, 0:512].astype(BF16)
        o_ref[rs, 1536:2560] = w_ref[0, rs, 1544:2568].astype(BF16)
        o_ref[rs, 2560:2816] = w_ref[0, rs, 2636:2892].astype(BF16)
        small = jnp.concatenate([w_ref[0, rs, 2568:2632], w_ref[0, rs, 1536:1544], w_ref[0, rs, 2632:2636],
                                 jnp.zeros((rows, SLAB_W - 2892), F32)], axis=1)
        o_ref[rs, 2816:SLAB_W] = small.astype(BF16)
        return 0

    lax.fori_loop(0, d // rows, body, 0)


def _slab_weight(w_in, l):
    _, d, n = w_in.shape
    return pl.pallas_call(
        _slab_weight_kernel,
        grid=(1,),
        in_specs=[pl.BlockSpec((1, d, n), lambda i: (l, 0, 0))],
        out_specs=pl.BlockSpec((d, SLAB_W), lambda i: (0, 0)),
        out_shape=jax.ShapeDtypeStruct((d, SLAB_W), BF16),
        compiler_params=_cparams("arbitrary"),
        name="slab_weight",
    )(w_in)


def _ssd_kernel(xbc_ref, z_ref, sm_ref, cw_ref, cb_ref, dtb_ref, alog_ref, dsk_ref, ng_ref,
                o_ref, st_ref, carry_ref, *, ts):
    q = SSD_CHUNK
    p = HEAD_DIM

    @pl.when(pl.program_id(1) == 0)
    def _():
        st_ref[...] = jnp.zeros_like(st_ref)
        carry_ref[...] = jnp.zeros_like(carry_ref)

    cw = cw_ref[...]
    cb = cb_ref[...]
    dtb = dtb_ref[...]
    a_row = -jnp.exp(alog_ref[...])
    dsk = dsk_ref[...]
    ng = ng_ref[...]
    row8 = lax.broadcasted_iota(I32, (8, SSD_CONV_DIM), 0)
    rowi = lax.broadcasted_iota(I32, (q, q), 0)
    coli = lax.broadcasted_iota(I32, (q, q), 1)
    tril = rowi >= coli

    def chunk(c, _):
        off = pl.multiple_of(c * q, q)
        cur = xbc_ref[pl.ds(off, q), :]
        poff = pl.multiple_of(jnp.maximum(off - 8, 0), 8)
        prev8 = jnp.where(c == 0, carry_ref[...], xbc_ref[pl.ds(poff, 8), :])
        acc = cur * cw[SSD_CONV - 1:SSD_CONV, :] + cb
        for k in range(1, SSD_CONV):
            rk = pltpu.roll(cur, k, 0)
            pk = pltpu.roll(prev8, k, 0)
            top = jnp.where(row8 < k, pk, rk[0:8, :])
            sh = jnp.concatenate([top, rk[8:, :]], axis=0)
            acc = acc + sh * cw[SSD_CONV - 1 - k:SSD_CONV - k, :]
        xc = _silu(acc)
        xs = xc[:, :SSD_INNER]

        sm = sm_ref[pl.ds(off, q), :]
        dtv = _softplus(sm + dtb)
        cs = dtv * a_row
        s = 1
        while s < q:
            cs = cs + jnp.where(rowi >= s, pltpu.roll(cs, s, 0), 0.0)
            s *= 2
        cst = cs.T

        ys = []
        for g in range(SSD_GROUPS):
            b0 = SSD_INNER + g * SSD_STATE
            c0 = SSD_INNER + SSD_GROUPS * SSD_STATE + g * SSD_STATE
            bm = xc[:, b0:b0 + SSD_STATE]
            cmb = xc[:, c0:c0 + SSD_STATE].astype(BF16)
            cbm = _nt(cmb, bm.astype(BF16))
            bmt = bm.T.astype(BF16)
            for r in range(SSD_HEADS // SSD_GROUPS):
                h = g * (SSD_HEADS // SSD_GROUPS) + r
                hl = SMALL_DT + h
                colb = jnp.broadcast_to(cs[:, hl:hl + 1], (q, q))
                seg = colb - cst[hl:hl + 1, :]
                lm = jnp.exp(jnp.where(tril, seg, -jnp.inf))
                xs_h = xs[:, h * p:(h + 1) * p]
                xdt = xs_h * jnp.broadcast_to(dtv[:, hl:hl + 1], (q, p))
                yd = _mm((cbm * lm).astype(BF16), xdt.astype(BF16))
                cb64 = colb[:, :p]
                last = jnp.broadcast_to(cb64[q - 1:q, :], (q, p))
                xw = (xdt * jnp.exp(last - cb64)).astype(BF16)
                st_new = _mm(bmt, xw)
                st_prev = st_ref[h]
                yo = _mm(cmb, st_prev.astype(BF16)) * jnp.exp(cb64)
                st_ref[h] = st_prev * jnp.exp(last) + st_new
                ys.append(yd + yo + xs_h * dsk[:, h * p:(h + 1) * p])
        y = jnp.concatenate(ys, axis=1)
        yg = y * _silu(z_ref[pl.ds(off, q), :])
        gw = SSD_INNER // SSD_GROUPS
        outs = []
        for g in range(SSD_GROUPS):
            v = yg[:, g * gw:(g + 1) * gw]
            rr = lax.rsqrt(jnp.mean(v * v, axis=-1, keepdims=True) + EPS)
            outs.append((v * rr) * ng[:, g * gw:(g + 1) * gw])
        o_ref[pl.ds(off, q), :] = jnp.concatenate(outs, axis=1).astype(o_ref.dtype)
        return 0

    lax.fori_loop(0, ts // q, chunk, 0)
    carry_ref[...] = xbc_ref[ts - 8:ts, :]


def _ssd(slab, B, L, conv_w, conv_b, dt_bias, a_log, d_skip, norm_g):
    T = B * L
    ts = min(L, 1024)
    nj = L // ts
    dtb = jnp.zeros((1, 128), F32).at[0, SMALL_DT:SMALL_DT + SSD_HEADS].set(dt_bias)
    alog = jnp.zeros((1, 128), F32).at[0, SMALL_DT:SMALL_DT + SSD_HEADS].set(a_log)
    dsk = jnp.repeat(d_skip, HEAD_DIM).reshape(1, SSD_INNER)
    row = lambda w: pl.BlockSpec((1, w), lambda b, j: (0, 0))
    return pl.pallas_call(
        functools.partial(_ssd_kernel, ts=ts),
        grid=(B, nj),
        in_specs=[pl.BlockSpec((ts, SSD_CONV_DIM), lambda b, j: (b * nj + j, BLK_XBC)),
                  pl.BlockSpec((ts, SSD_INNER), lambda b, j: (b * nj + j, BLK_Z)),
                  pl.BlockSpec((ts, 128), lambda b, j: (b * nj + j, BLK_SMALL)),
                  pl.BlockSpec((SSD_CONV, SSD_CONV_DIM), lambda b, j: (0, 0)),
                  row(SSD_CONV_DIM), row(128), row(128), row(SSD_INNER), row(SSD_INNER)],
        out_specs=pl.BlockSpec((ts, SSD_INNER), lambda b, j: (b * nj + j, 0)),
        out_shape=jax.ShapeDtypeStruct((T, SSD_INNER), BF16),
        scratch_shapes=[pltpu.VMEM((SSD_HEADS, SSD_STATE, HEAD_DIM), F32),
                        pltpu.VMEM((8, SSD_CONV_DIM), F32)],
        compiler_params=_cparams("arbitrary", "arbitrary"),
        name="ssd",
    )(slab, slab, slab, conv_w, conv_b.reshape(1, -1), dtb, alog, dsk, norm_g.reshape(1, -1))


def _prep_kernel(q_ref, k_ref, v_ref, iq_ref, mq_ref, sm_ref, cos_ref, sin_ref,
                 qg_ref, kg_ref, mg_ref,
                 qn_ref, kn_ref, vt_ref, iqr_ref, ika_ref, ikb_ref, iwt_ref, mqn_ref):
    cos = cos_ref[...]
    sin_s = sin_ref[...]
    seg = _seg_ones(ATTN_INNER)
    scale = (HEAD_DIM ** -0.5) * LOG2_E
    qn = _rope(_head_rms(q_ref[...], seg) * qg_ref[...], cos, sin_s) * scale
    qn_ref[...] = qn.astype(BF16)
    kn = _rope(_head_rms(k_ref[...], seg) * kg_ref[...], cos, sin_s)
    kn_ref[...] = kn.astype(BF16)
    for c in range(vt_ref.shape[0]):
        vt_ref[c] = v_ref[c * KEY_CHUNK:(c + 1) * KEY_CHUNK, :].T.astype(BF16)
    iqr_ref[...] = _rope(iq_ref[...], cos, sin_s).astype(BF16)
    mqn_ref[...] = ((_head_rms(mq_ref[...], seg) * mg_ref[...]) * scale).astype(BF16)
    sm = sm_ref[...]
    ikr = _rope(sm, cos, sin_s)
    lane = lax.broadcasted_iota(I32, ikr.shape, 1)
    ika = jnp.where(lane < IDX_DIM, ikr, 0.0)
    ika_ref[...] = ika.astype(BF16)
    ikb_ref[...] = pltpu.roll(ika, IDX_DIM, 1).astype(BF16)
    smt = sm.T
    iwt_ref[...] = smt[SMALL_IW:SMALL_IW + 8, :] * ((IDX_HEADS ** -0.5) * (IDX_DIM ** -0.5))


def _prep(slab, cos_t, sin_t, qg, kg, mg):
    T = slab.shape[0]
    tm = min(T, 2 * KEY_CHUNK)
    tile4 = lambda g: jnp.tile(g, ATTN_INNER // HEAD_DIM).reshape(1, ATTN_INNER)
    col = lambda w, blk: pl.BlockSpec((tm, w), lambda i: (i, blk))
    row = lambda w: pl.BlockSpec((1, w), lambda i: (0, 0))
    out = lambda w: pl.BlockSpec((tm, w), lambda i: (i, 0))
    return pl.pallas_call(
        _prep_kernel,
        grid=(T // tm,),
        in_specs=[col(256, BLK_Q), col(256, BLK_K), col(256, BLK_V), col(256, BLK_IQ),
                  col(256, BLK_MQ), col(128, BLK_SMALL), out(128), out(128),
                  row(256), row(256), row(256)],
        out_specs=[out(256), out(256), pl.BlockSpec((tm // KEY_CHUNK, 256, KEY_CHUNK), lambda i: (i, 0, 0)),
                   out(256), out(128), out(128), pl.BlockSpec((8, tm), lambda i: (0, i)), out(256)],
        out_shape=[jax.ShapeDtypeStruct((T, 256), BF16), jax.ShapeDtypeStruct((T, 256), BF16),
                   jax.ShapeDtypeStruct((T // KEY_CHUNK, 256, KEY_CHUNK), BF16),
                   jax.ShapeDtypeStruct((T, 256), BF16), jax.ShapeDtypeStruct((T, 128), BF16),
                   jax.ShapeDtypeStruct((T, 128), BF16), jax.ShapeDtypeStruct((8, T), F32),
                   jax.ShapeDtypeStruct((T, 256), BF16)],
        compiler_params=_cparams("parallel"),
        name="attn_prep",
    )(slab, slab, slab, slab, slab, slab, cos_t, sin_t, tile4(qg), tile4(kg), tile4(mg))


def _mem_kv_kernel(mem_ref, g_ref, w_ref, kg_ref, mk_ref, mvt_ref):
    x = mem_ref[...]
    r = lax.rsqrt(jnp.mean(x * x, axis=-1, keepdims=True) + EPS)
    h = ((x * r) * g_ref[...]).astype(BF16)
    kv = _mm(h, w_ref[...])
    seg = _seg_ones(MEM_INNER)
    mk_ref[...] = (_head_rms(kv[:, :MEM_INNER], seg) * kg_ref[...]).astype(BF16)
    mvt_ref[0] = kv[:, MEM_INNER:].T.astype(BF16)


def _mem_kv(mem, g, w, kg):
    B, M, D = mem.shape
    return pl.pallas_call(
        _mem_kv_kernel,
        grid=(B,),
        in_specs=[pl.BlockSpec((M, D), lambda b: (b, 0)),
                  pl.BlockSpec((1, D), lambda b: (0, 0)),
                  pl.BlockSpec((D, 2 * MEM_INNER), lambda b: (0, 0)),
                  pl.BlockSpec((1, MEM_INNER), lambda b: (0, 0))],
        out_specs=[pl.BlockSpec((M, MEM_INNER), lambda b: (b, 0)),
                   pl.BlockSpec((1, MEM_INNER, M), lambda b: (b, 0, 0))],
        out_shape=[jax.ShapeDtypeStruct((B * M, MEM_INNER), BF16),
                   jax.ShapeDtypeStruct((B, MEM_INNER, M), BF16)],
        compiler_params=_cparams("parallel"),
        name="mem_kv",
    )(mem.reshape(B * M, D), g.reshape(1, D), w.astype(BF16),
      jnp.tile(kg, MEM_INNER // HEAD_DIM).reshape(1, MEM_INNER))


def _split_heads(x):
    lane = lax.broadcasted_iota(I32, (Q_BLOCK, 128), 1)
    lo = lane < HEAD_DIM
    zero = jnp.zeros((Q_BLOCK, 128), x.dtype)
    heads = []
    for pr in range(2):
        xp = x[:, pr * 128:(pr + 1) * 128]
        heads += [jnp.where(lo, xp, zero), jnp.where(lo, zero, xp)]
    return heads


def _attn_kernel(qn_ref, iqr_ref, iwt_ref, mqn_ref, ika_ref, ikb_ref, kn_ref, vt_ref, mk_ref, mvt_ref,
                 o_ref, keys_ref, khi_ref, klo_ref, bias_ref, sm_ref, pb_ref, *, topk):
    kc = KEY_CHUNK
    qb = Q_BLOCK
    nh = ATTN_INNER // HEAD_DIM
    i = pl.program_id(1)
    q0 = i * qb
    nch = (q0 + qb + kc - 1) // kc
    iw = iwt_ref[...]
    iq = iqr_ref[...]
    rel = (lax.broadcasted_iota(I32, (kc, qb), 0) - lax.broadcasted_iota(I32, (kc, qb), 1)) - q0

    def chunk_off(c):
        return pl.multiple_of(c * kc, kc)

    def p1(c, _):
        off = chunk_off(c)
        ika = ika_ref[pl.ds(off, kc), :]
        ikb = ikb_ref[pl.ds(off, kc), :]
        isc = iw[0:1, :] * jnp.maximum(_nt(ika, iq[:, :128]), 0.0)
        isc = isc + iw[1:2, :] * jnp.maximum(_nt(ikb, iq[:, :128]), 0.0)
        isc = isc + iw[2:3, :] * jnp.maximum(_nt(ika, iq[:, 128:]), 0.0)
        isc = isc + iw[3:4, :] * jnp.maximum(_nt(ikb, iq[:, 128:]), 0.0)
        isc = jnp.where(rel <= -off, isc, -jnp.inf)
        bits = pltpu.bitcast(isc, I32)
        ks = jnp.where(bits < 0, bits ^ 0x7FFFFFFF, bits)
        keys_ref[pl.ds(off, kc), :] = ks
        khi_ref[pl.ds(off, kc), :] = (ks >> 16).astype(I16)
        klo_ref[pl.ds(off, kc), :] = (ks ^ 0x8000).astype(I16)
        return 0

    lax.fori_loop(0, nch, p1, 0)

    one_b = jnp.ones((16, qb), BF16)
    zero_b = jnp.zeros((16, qb), BF16)

    def count16(ref, cand, strict):
        def body(c, acc):
            off = chunk_off(c)
            parts = [zero_b] * COUNT_LANES
            for r in range(kc // 16):
                kk = ref[pl.ds(off + r * 16, 16), :]
                m = jnp.where(kk > cand if strict else kk >= cand, one_b, zero_b)
                parts[r % COUNT_LANES] = parts[r % COUNT_LANES] + m
            while len(parts) > 1:
                parts = [parts[r] + parts[r + 1] for r in range(0, len(parts), 2)]
            return acc + parts[0].astype(F32)
        acc = lax.fori_loop(0, nch, body, jnp.zeros((16, qb), F32))
        return jnp.sum(acc, axis=0, keepdims=True)

    def kth_largest16(ref, need):
        def bis(it, t_u):
            cand_u = t_u | jnp.left_shift(jnp.int32(1), 15 - it)
            cnt = count16(ref, (cand_u - 32768).astype(I16), False)
            return jnp.where(cnt >= need, cand_u, t_u)
        return lax.fori_loop(0, 16, bis, jnp.zeros((1, qb), I32)) - 32768

    t_hi = kth_largest16(khi_ref, float(topk))
    t_hi16 = t_hi.astype(I16)
    gt_hi = count16(khi_ref, t_hi16, True)

    def band(c, _):
        off = chunk_off(c)
        klo_ref[pl.ds(off, kc), :] = jnp.where(khi_ref[pl.ds(off, kc), :] == t_hi16,
                                               klo_ref[pl.ds(off, kc), :], jnp.int16(-32768))
        return 0

    lax.fori_loop(0, nch, band, 0)
    t_lo = kth_largest16(klo_ref, topk - gt_hi)
    gt_lo = count16(klo_ref, t_lo.astype(I16), True)
    t = t_hi * 65536 + (t_lo + 32768)
    r_eq = jnp.where(t == KEY_NEG_INF, 0.0, topk - (gt_hi + gt_lo))

    q_heads = _split_heads(qn_ref[...])
    tb = TIE_BLOCK
    tri = jnp.where(lax.broadcasted_iota(I32, (tb, tb), 0) >= lax.broadcasted_iota(I32, (tb, tb), 1),
                    1.0, 0.0).astype(BF16)

    def p3(c, carry):
        eq_cnt, ms, ls, accs = carry
        off = chunk_off(c)
        for r in range(kc // tb):
            k = keys_ref[pl.ds(off + r * tb, tb), :]
            eq = k == t
            pref = _mm(tri, jnp.where(eq, 1.0, 0.0).astype(BF16)) + eq_cnt
            eq_cnt = pref[tb - 1:tb, :]
            tie_bias = jnp.where(pref <= r_eq, 0.0, -jnp.inf)
            bias_ref[r * tb:(r + 1) * tb, :] = jnp.where(k > t, 0.0, jnp.where(eq, tie_bias, -jnp.inf))
        kn = kn_ref[pl.ds(off, kc), :]
        m_news, alphas = [], []
        for h in range(nh):
            s = _nt(kn[:, (h // 2) * 128:(h // 2 + 1) * 128], q_heads[h]) + bias_ref[...]
            sm_ref[h] = s
            cm = jnp.max(jnp.max(s.reshape(kc // 8, 8, qb), axis=0), axis=0, keepdims=True)
            m_new = jnp.maximum(ms[h], cm)
            m_news.append(m_new)
            alphas.append(jnp.exp2(ms[h] - m_new))
        ls2 = []
        for h in range(nh):
            pexp = jnp.exp2(sm_ref[h] - m_news[h])
            psum = jnp.sum(jnp.sum(pexp.reshape(kc // 8, 8, qb), axis=0), axis=0, keepdims=True)
            ls2.append(ls[h] * alphas[h] + psum)
            pb_ref[h] = pexp.astype(BF16)
        vt = vt_ref[c]
        accs2 = [accs[h] * alphas[h] + _mm(vt[h * HEAD_DIM:(h + 1) * HEAD_DIM, :], pb_ref[h])
                 for h in range(nh)]
        return eq_cnt, tuple(m_news), tuple(ls2), tuple(accs2)

    init = (jnp.zeros((1, qb), F32),
            tuple(jnp.full((1, qb), NEG_BIG, F32) for _ in range(nh)),
            tuple(jnp.zeros((1, qb), F32) for _ in range(nh)),
            tuple(jnp.zeros((HEAD_DIM, qb), F32) for _ in range(nh)))
    _, _, ls, accs = lax.fori_loop(0, nch, p3, init)
    outs = [accs[h] * (1.0 / ls[h]) for h in range(nh)]

    m_heads = _split_heads(mqn_ref[...])
    mk = mk_ref[...]
    mvt = mvt_ref[0]
    for h in range(MEM_INNER // HEAD_DIM):
        s = _nt(mk[:, (h // 2) * 128:(h // 2 + 1) * 128], m_heads[h])
        mx = jnp.max(s, axis=0, keepdims=True)
        pexp = jnp.exp2(s - mx)
        den = jnp.sum(pexp, axis=0, keepdims=True)
        o = _mm(mvt[h * HEAD_DIM:(h + 1) * HEAD_DIM, :], pexp.astype(BF16))
        outs.append(o * (1.0 / den))
    o_ref[...] = jnp.concatenate(outs, axis=0).T.astype(o_ref.dtype)


def _attention(qn, iqr, iwt, mqn, ika, ikb, kn, vt, mk, mvt, B, L):
    T = B * L
    nq = L // Q_BLOCK
    ncl = L // KEY_CHUNK
    topk = min(TOPK_MAX, L // 4)
    assert KEY_CHUNK >= topk and KEY_CHUNK % TIE_BLOCK == 0
    nh = ATTN_INNER // HEAD_DIM
    qblk = lambda w: pl.BlockSpec((Q_BLOCK, w), lambda b, i: (b * nq + i, 0))
    kblk = lambda w: pl.BlockSpec((L, w), lambda b, i: (b, 0))
    return pl.pallas_call(
        functools.partial(_attn_kernel, topk=topk),
        grid=(B, nq),
        in_specs=[qblk(256), qblk(256), pl.BlockSpec((8, Q_BLOCK), lambda b, i: (0, b * nq + i)), qblk(256),
                  kblk(128), kblk(128), kblk(256),
                  pl.BlockSpec((ncl, 256, KEY_CHUNK), lambda b, i: (b, 0, 0)),
                  pl.BlockSpec((N_MEM, MEM_INNER), lambda b, i: (b, 0)),
                  pl.BlockSpec((1, MEM_INNER, N_MEM), lambda b, i: (b, 0, 0))],
        out_specs=pl.BlockSpec((Q_BLOCK, ATTN_INNER + MEM_INNER), lambda b, i: (b * nq + i, 0)),
        out_shape=jax.ShapeDtypeStruct((T, ATTN_INNER + MEM_INNER), BF16),
        scratch_shapes=[pltpu.VMEM((L, Q_BLOCK), I32), pltpu.VMEM((L, Q_BLOCK), I16),
                        pltpu.VMEM((L, Q_BLOCK), I16), pltpu.VMEM((KEY_CHUNK, Q_BLOCK), F32),
                        pltpu.VMEM((nh, KEY_CHUNK, Q_BLOCK), F32),
                        pltpu.VMEM((nh, KEY_CHUNK, Q_BLOCK), BF16)],
        compiler_params=_cparams("parallel", "arbitrary"),
        name="attention",
    )(qn, iqr, iwt, mqn, ika, ikb, kn, vt, mk, mvt)


def _out_proj_kernel(x_ref, ys_ref, ya_ref, w_ref, g_ref, x1_ref, hn_ref):
    w = w_ref[...]
    x1 = x_ref[...] + _mm(ys_ref[...], w[:SSD_INNER, :]) + _mm(ya_ref[...], w[SSD_INNER:, :])
    x1_ref[...] = x1
    r = lax.rsqrt(jnp.mean(x1 * x1, axis=-1, keepdims=True) + EPS)
    hn_ref[...] = ((x1 * r) * g_ref[...]).astype(BF16)


def _out_proj(xf, y_ssd, y_att, w_out, l, g):
    T, D = xf.shape
    tm = min(T, 1024)
    blk = lambda w: pl.BlockSpec((tm, w), lambda i: (i, 0))
    return pl.pallas_call(
        _out_proj_kernel,
        grid=(T // tm,),
        in_specs=[blk(D), blk(SSD_INNER), blk(ATTN_INNER + MEM_INNER),
                  pl.BlockSpec((None, D, D), lambda i: (l, 0, 0)), pl.BlockSpec((1, D), lambda i: (0, 0))],
        out_specs=[blk(D), blk(D)],
        out_shape=[jax.ShapeDtypeStruct((T, D), F32), jax.ShapeDtypeStruct((T, D), BF16)],
        compiler_params=_cparams("parallel"),
        name="out_proj",
    )(xf, y_ssd, y_att, w_out, g.reshape(1, D))


def _ffn_kernel(h_ref, halo_ref, wg0_ref, wv0_ref, wgn_ref, wvn_ref, cwg_ref, cwv_ref, cbg_ref, cbv_ref,
                wdp_ref, wdl_ref, x1_ref, o_ref,
                uga_ref, uva_ref, ugb_ref, uvb_ref, aa_ref, ab_ref, acc_ref, *, tiles_per_seq, nf):
    i = pl.program_id(0)
    j = pl.program_id(1)
    halo = halo_ref[...]
    halo = jnp.where(i % tiles_per_seq == 0, jnp.zeros_like(halo), halo)
    tm = h_ref.shape[0]
    rs = FFN_ROWS
    cwg = cwg_ref[...]
    cwv = cwv_ref[...]
    cbg = cbg_ref[...]
    cbv = cbv_ref[...]

    def up(w, u_ref, r0, nrows):
        if r0 == 0:
            u_ref[0:8, :] = _mm(halo, w)
        u_ref[8 + r0:8 + r0 + nrows, :] = _mm(h_ref[r0:r0 + nrows, :], w)

    def conv(u_ref, r0, cw, cb):
        blk = u_ref[r0:r0 + rs + 8, :]
        acc = blk[8:, :] * cw[FFN_CONV - 1:FFN_CONV, :] + cb
        for k in range(1, FFN_CONV):
            acc = acc + blk[8 - k:8 - k + rs, :] * cw[FFN_CONV - 1 - k:FFN_CONV - k, :]
        return acc

    @pl.when(j == 0)
    def _():
        up(wg0_ref[...], uga_ref, 0, tm)
        up(wv0_ref[...], uva_ref, 0, tm)
        ab_ref[...] = jnp.zeros_like(ab_ref)
        acc_ref[...] = jnp.zeros_like(acc_ref)

    def stage(u_cur, u_nxt, a_cur, a_prv):
        qr = tm // FFN_SPLIT
        wgn = wgn_ref[...]
        wvn = wvn_ref[...]
        wdp = wdp_ref[...]
        for q in range(FFN_SPLIT):
            up(wgn, u_nxt[0], q * qr, qr)
            up(wvn, u_nxt[1], q * qr, qr)
            for r0 in range(q * qr, (q + 1) * qr, rs):
                gate = conv(u_cur[0], r0, cwg, cbg)
                val = conv(u_cur[1], r0, cwv, cbv)
                a_cur[r0:r0 + rs, :] = (_silu(gate) * val).astype(BF16)
            acc_ref[q * qr:(q + 1) * qr, :] += _mm(a_prv[q * qr:(q + 1) * qr, :], wdp)

    @pl.when(j % 2 == 0)
    def _():
        stage((uga_ref, uva_ref), (ugb_ref, uvb_ref), aa_ref, ab_ref)

    @pl.when(j % 2 == 1)
    def _():
        stage((ugb_ref, uvb_ref), (uga_ref, uva_ref), ab_ref, aa_ref)

    @pl.when(j == nf - 1)
    def _():
        a_last = aa_ref if (nf - 1) % 2 == 0 else ab_ref
        o_ref[...] = x1_ref[...] + (acc_ref[...] + _mm(a_last[...], wdl_ref[...]))


def _ffn(x1, hn, wb, conv_w, conv_b, wd, l, L):
    T, D = x1.shape
    tm = min(L, 1024)
    nf = D_FF // FF_TILE
    cb = conv_b.reshape(1, 2 * D_FF)
    hb = tm // 8
    nxt = lambda j: jnp.minimum(j + 1, nf - 1)
    wspec = lambda f: pl.BlockSpec((None, D, FF_TILE), f)
    u_scr = pltpu.VMEM((tm + 8, FF_TILE), F32)
    a_scr = pltpu.VMEM((tm, FF_TILE), BF16)
    return pl.pallas_call(
        functools.partial(_ffn_kernel, tiles_per_seq=L // tm, nf=nf),
        grid=(T // tm, nf),
        in_specs=[pl.BlockSpec((tm, D), lambda i, j: (i, 0)),
                  pl.BlockSpec((8, D), lambda i, j: (jnp.maximum(i * hb - 1, 0), 0)),
                  wspec(lambda i, j: (l, 0, 0)), wspec(lambda i, j: (l, 0, nf)),
                  wspec(lambda i, j: (l, 0, nxt(j))), wspec(lambda i, j: (l, 0, nf + nxt(j))),
                  pl.BlockSpec((FFN_CONV, FF_TILE), lambda i, j: (0, j)),
                  pl.BlockSpec((FFN_CONV, FF_TILE), lambda i, j: (0, nf + j)),
                  pl.BlockSpec((1, FF_TILE), lambda i, j: (0, j)),
                  pl.BlockSpec((1, FF_TILE), lambda i, j: (0, nf + j)),
                  pl.BlockSpec((None, FF_TILE, D), lambda i, j: (l, jnp.maximum(j - 1, 0), 0)),
                  pl.BlockSpec((None, FF_TILE, D), lambda i, j: (l, nf - 1, 0)),
                  pl.BlockSpec((tm, D), lambda i, j: (i, 0))],
        out_specs=pl.BlockSpec((tm, D), lambda i, j: (i, 0)),
        out_shape=jax.ShapeDtypeStruct((T, D), F32),
        scratch_shapes=[u_scr, u_scr, u_scr, u_scr, a_scr, a_scr, pltpu.VMEM((tm, D), F32)],
        compiler_params=_cparams("parallel", "arbitrary"),
        name="conv_ffn",
    )(hn, hn, wb, wb, wb, wb, conv_w, conv_w, cb, cb, wd, wd, x1)


def kernel(x, mem, pos, mix_norm_g, w_in, ssd_conv_w, ssd_conv_b, ssd_dt_bias, ssd_a_log, ssd_d,
           ssd_norm_g, attn_q_norm_g, attn_k_norm_g, mem_norm_g, w_mem_kv, mem_q_norm_g,
           mem_k_norm_g, w_out, ffn_norm_g, w_up, ffn_conv_w, ffn_conv_b, w_down):
    B, L, D = x.shape
    T = B * L
    depth = w_in.shape[0]
    assert L % max(KEY_CHUNK, 1024 if L >= 1024 else KEY_CHUNK) == 0 and D == 1024
    xf = x.reshape(T, D)
    cos_t, sin_t = _rope_tables(pos)
    w_out_b, w_up_b, w_down_b = w_out.astype(BF16), w_up.astype(BF16), w_down.astype(BF16)
    for l in range(depth):
        slab = _in_proj(xf, mix_norm_g[l], _slab_weight(w_in, l))
        y_ssd = _ssd(slab, B, L, ssd_conv_w[l], ssd_conv_b[l], ssd_dt_bias[l], ssd_a_log[l],
                     ssd_d[l], ssd_norm_g[l])
        qn, kn, vt, iqr, ika, ikb, iwt, mqn = _prep(slab, cos_t, sin_t, attn_q_norm_g[l],
                                                    attn_k_norm_g[l], mem_q_norm_g[l])
        mk, mvt = _mem_kv(mem, mem_norm_g[l], w_mem_kv[l], mem_k_norm_g[l])
        y_att = _attention(qn, iqr, iwt, mqn, ika, ikb, kn, vt, mk, mvt, B, L)
        x1, hn = _out_proj(xf, y_ssd, y_att, w_out_b, l, ffn_norm_g[l])
        xf = _ffn(x1, hn, w_up_b, ffn_conv_w[l], ffn_conv_b[l], w_down_b, l, L)
    return xf.reshape(B, L, D)
```
